```python
import jax, jax.numpy as jnp
from jax import lax
import numpy as np

D_MODEL = 1024
BATCH = 4
SEQ = 4096
DEPTH = 4
DEC_BATCH = 128
DEC_SEQ = 8
PAST_LEN = 2048
PAGE_SIZE = 128

N_A = DEPTH // 2
N_B = DEPTH - N_A
CONV_W = 31
N_HEADS = 16
N_KV = 4
GROUP = N_HEADS // N_KV
HEAD_DIM = D_MODEL // N_HEADS
KV_W = N_KV * HEAD_DIM
Q_W = N_HEADS * HEAD_DIM
L_CMP = 32
STRIDE = 16
CMP_R = L_CMP // STRIDE
CMP_HID = 2 * HEAD_DIM
L_SEL = 64
TOP_N = 16
WINDOW = 512
Q_BLK = 128
D_FF = (8 * D_MODEL + 3 * 256 - 1) // (3 * 256) * 256
N_MOD = 6 * DEPTH + 4
EPS = 1e-6
FORCE = 1e3
NEG = -1e30

kernel_name = 'yoco_conformer_nsa_decoder_step'


def rmsnorm(x, g):
    xf = x.astype(jnp.float32)
    y = xf * lax.rsqrt(jnp.mean(xf * xf, axis=-1, keepdims=True) + EPS)
    return (y * g.astype(jnp.float32)).astype(x.dtype)


def modulate(x, g, shift, scale):
    return rmsnorm(x, g) * (1 + scale[:, None, :]) + shift[:, None, :]


def layernorm(x, g, b):
    xf = x.astype(jnp.float32)
    mu = jnp.mean(xf, axis=-1, keepdims=True)
    var = jnp.mean(jnp.square(xf - mu), axis=-1, keepdims=True)
    y = (xf - mu) * lax.rsqrt(var + EPS) * g.astype(jnp.float32) + b.astype(jnp.float32)
    return y.astype(x.dtype)


def alibi_slopes():
    return jnp.exp2(-8.0 * jnp.arange(1, N_HEADS + 1, dtype=jnp.float32) / N_HEADS)


def masked_softmax(s, mask):
    p = jax.nn.softmax(jnp.where(mask, s, NEG), axis=-1)
    return jnp.where(mask, p, 0.0)


def swiglu(h, w_up, w_down):
    a = h @ w_up
    return (jax.nn.silu(a[..., :D_FF]) * a[..., D_FF:]) @ w_down


def conv_module(h, prev, w_pw1, b_pw1, w_dw, b_dw, ln_g, ln_b, w_pw2):
    a = h @ w_pw1 + b_pw1
    u = a[..., :D_MODEL] * jax.nn.sigmoid(a[..., D_MODEL:])
    up = jnp.concatenate([prev.astype(u.dtype), u], axis=1)
    y = lax.conv_general_dilated(up, w_dw[:, None, :].astype(up.dtype), (1,), 'VALID',
                                 dimension_numbers=('NWC', 'WIO', 'NWC'),
                                 feature_group_count=D_MODEL) + b_dw
    y = jax.nn.silu(layernorm(y, ln_g, ln_b))
    return y @ w_pw2, up[:, -(CONV_W - 1):]


def compress(x, w1, w2, pe):
    B, T = x.shape[0], x.shape[1]
    n_c = (T - L_CMP) // STRIDE + 1
    n_half = -(-T // STRIDE)
    xp = jnp.pad(x, ((0, 0), (0, n_half * STRIDE - T), (0, 0), (0, 0)))
    xp = xp.reshape(B, n_half, STRIDE, N_KV, HEAD_DIM)
    parts = []
    for r in range(CMP_R):
        seg = slice(r * STRIDE, (r + 1) * STRIDE)
        pr = jnp.einsum('bnlkd,lde->bnke', xp, w1[seg]) + jnp.einsum('ld,lde->e', pe[seg], w1[seg])
        parts.append(pr[:, r:r + n_c])
    hid = sum(parts[1:], parts[0])
    return jnp.einsum('bnke,ed->bnkd', jax.nn.silu(hid), w2)


def sel_blocks(x):
    B, T = x.shape[0], x.shape[1]
    n_s = -(-T // L_SEL)
    xp = jnp.pad(x, ((0, 0), (0, n_s * L_SEL - T), (0, 0), (0, 0)))
    return xp.reshape(B, n_s, L_SEL, N_KV, HEAD_DIM).transpose(0, 3, 1, 2, 4)


def build_ctx(kv4, w_cmp1, w_cmp2, pe_cmp):
    kc = compress(kv4[:, :, 0], w_cmp1[0], w_cmp2[0], pe_cmp[0])
    vc = compress(kv4[:, :, 1], w_cmp1[1], w_cmp2[1], pe_cmp[1])
    ksb = sel_blocks(kv4[:, :, 2])
    vsb = sel_blocks(kv4[:, :, 3])
    n_c, n_s = kc.shape[1], ksb.shape[2]
    cend = jnp.arange(n_c, dtype=jnp.int32) * STRIDE + (L_CMP - 1)
    i0 = jnp.arange(n_c)[:, None] * STRIDE
    j0 = jnp.arange(n_s)[None, :] * L_SEL
    m_sel = ((i0 < j0 + L_SEL) & (i0 + L_CMP > j0)).astype(jnp.float32)
    return (kc, vc, cend, m_sel, ksb, vsb)


def nsa_core(q, gate, qpos, ctx, kw, vw, kwpos, slopes):
    kc, vc, cend, m_sel, ksb, vsb = ctx
    B, Q = q.shape[0], q.shape[1]
    scale = HEAD_DIM ** -0.5
    sl = slopes.reshape(1, 1, N_KV, GROUP, 1)
    d_c = (qpos[:, None] - cend[None, :]).astype(jnp.float32)[None, :, None, None, :]
    s_c = jnp.einsum('bqkgd,bnkd->bqkgn', q, kc).astype(jnp.float32) * scale - sl * d_c
    p_c = masked_softmax(s_c, d_c >= 0)
    o_c = jnp.einsum('bqkgn,bnkd->bqkgd', p_c.astype(vc.dtype), vc)
    n_s = m_sel.shape[1]
    imp = jnp.einsum('bqkgn,ns->bqks', p_c, m_sel)
    blk = jnp.arange(n_s, dtype=jnp.int32)[None, :]
    cur = (qpos // L_SEL)[:, None]
    valid = (blk * L_SEL <= qpos[:, None])[None, :, None, :]
    forced = ((blk == 0) | (blk == cur) | (blk == cur - 1))[None, :, None, :]
    score = jnp.where(valid, imp + jnp.where(forced, FORCE, 0.0), -jnp.inf)
    _, idx = lax.top_k(score, min(TOP_N, n_s))
    bi = jnp.arange(B)[:, None, None, None]
    ki = jnp.arange(N_KV)[None, None, :, None]
    ks = ksb[bi, ki, idx]
    vs = vsb[bi, ki, idx]
    pos = idx[..., None] * L_SEL + jnp.arange(L_SEL, dtype=jnp.int32)
    d_s = (qpos[None, :, None, None, None] - pos).astype(jnp.float32)[:, :, :, None]
    s_s = jnp.einsum('bqkgd,bqknld->bqkgnl', q, ks).astype(jnp.float32) * scale - sl[..., None] * d_s
    shp = s_s.shape
    p_s = masked_softmax(s_s.reshape(shp[:4] + (-1,)), (d_s >= 0).reshape(B, Q, N_KV, 1, -1)).reshape(shp)
    o_s = jnp.einsum('bqkgnl,bqknld->bqkgd', p_s.astype(vs.dtype), vs)
    d_w = qpos[:, None] - kwpos[None, :]
    m_w = ((d_w >= 0) & (d_w < WINDOW) & (kwpos >= 0)[None, :])[None, :, None, None, :]
    s_w = jnp.einsum('bqkgd,btkd->bqkgt', q, kw).astype(jnp.float32) * scale \
        - sl * d_w.astype(jnp.float32)[None, :, None, None, :]
    p_w = masked_softmax(s_w, m_w)
    o_w = jnp.einsum('bqkgt,btkd->bqkgd', p_w.astype(vw.dtype), vw)
    g = jax.nn.sigmoid(gate.astype(jnp.float32))
    o = g[..., 0:1] * o_c + g[..., 1:2] * o_s + g[..., 2:3] * o_w
    return o.reshape(B, Q, Q_W).astype(q.dtype)


def nsa_mixer(h, w_in, w_o, ctx, q_blk, pos0, window_fn, slopes):
    B, T = h.shape[0], h.shape[1]
    nb = T // q_blk
    a = h @ w_in
    q = a[..., :Q_W].reshape(B, nb, q_blk, N_KV, GROUP, HEAD_DIM).swapaxes(0, 1)
    g = a[..., Q_W:].reshape(B, nb, q_blk, N_KV, GROUP, 3).swapaxes(0, 1)
    s0 = pos0 + q_blk * jnp.arange(nb, dtype=jnp.int32)

    def one_block(args):
        qi, gi, si = args
        qpos = si + jnp.arange(q_blk, dtype=jnp.int32)
        kw, vw, kwpos = window_fn(si)
        return nsa_core(qi, gi, qpos, ctx, kw, vw, kwpos, slopes)

    o = lax.map(one_block, (q, g, s0))
    return o.swapaxes(0, 1).reshape(B, T, Q_W) @ w_o


def setup_inputs(seed: int = 0) -> dict:
    key = jax.random.key(seed)
    ks = jax.random.split(key, 28)

    def nrm(k, shape, s):
        return jax.random.normal(k, shape, jnp.float32) * s

    n_pages = PAST_LEN // PAGE_SIZE
    n_used = DEC_BATCH * n_pages
    n_pool = n_used + n_used // 4
    page_table = jax.random.permutation(ks[3], n_pool)[:n_used].reshape(DEC_BATCH, n_pages).astype(jnp.int32)
    return {
        'x_prompt': nrm(ks[0], (BATCH, SEQ, D_MODEL), 1.0),
        'x_sample': nrm(ks[1], (DEC_BATCH, DEC_SEQ, D_MODEL), 1.0),
        'cache_kv': nrm(ks[2], (n_pool, PAGE_SIZE, 4, N_KV, HEAD_DIM), 1.0),
        'page_table': page_table,
        'state_win': nrm(ks[4], (DEC_BATCH, min(WINDOW, PAST_LEN), 2, N_KV, HEAD_DIM), 1.0),
        'state_conv': nrm(ks[5], (N_A, DEC_BATCH, CONV_W - 1, D_MODEL), 0.5),
        'c_prompt': nrm(ks[6], (BATCH, D_MODEL), 1.0),
        'c_sample': nrm(ks[7], (DEC_BATCH, D_MODEL), 1.0),
        'w_ada': nrm(ks[8], (D_MODEL, N_MOD * D_MODEL), 0.5 * D_MODEL ** -0.5),
        'b_ada': nrm(ks[9], (N_MOD * D_MODEL,), 0.01),
        'norm_g': 1.0 + nrm(ks[10], (DEPTH, 2, D_MODEL), 0.02),
        'conv_w_pw1': nrm(ks[11], (N_A, D_MODEL, 2 * D_MODEL), D_MODEL ** -0.5),
        'conv_b_pw1': nrm(ks[12], (N_A, 2 * D_MODEL), 0.01),
        'conv_w_dw': nrm(ks[13], (N_A, CONV_W, D_MODEL), CONV_W ** -0.5),
        'conv_b_dw': nrm(ks[14], (N_A, D_MODEL), 0.01),
        'conv_ln_g': 1.0 + nrm(ks[15], (N_A, D_MODEL), 0.02),
        'conv_ln_b': nrm(ks[16], (N_A, D_MODEL), 0.01),
        'conv_w_pw2': nrm(ks[17], (N_A, D_MODEL, D_MODEL), D_MODEL ** -0.5),
        'ffn_w_up': nrm(ks[18], (DEPTH, D_MODEL, 2 * D_FF), D_MODEL ** -0.5),
        'ffn_w_down': nrm(ks[19], (DEPTH, D_FF, D_MODEL), D_FF ** -0.5),
        'kv_norm_g': 1.0 + nrm(ks[20], (D_MODEL,), 0.02),
        'w_kv': nrm(ks[21], (D_MODEL, 6 * KV_W), D_MODEL ** -0.5),
        'w_cmp1': nrm(ks[22], (2, L_CMP, HEAD_DIM, CMP_HID), (L_CMP * HEAD_DIM) ** -0.5),
        'w_cmp2': nrm(ks[23], (2, CMP_HID, HEAD_DIM), CMP_HID ** -0.5),
        'pe_cmp': nrm(ks[24], (2, L_CMP, HEAD_DIM), 0.1),
        'nsa_w_in': nrm(ks[25], (N_B, D_MODEL, Q_W + 3 * N_HEADS), D_MODEL ** -0.5),
        'nsa_w_o': nrm(ks[26], (N_B, Q_W, D_MODEL), Q_W ** -0.5),
        'final_norm_g': 1.0 + nrm(ks[27], (D_MODEL,), 0.02),
    }


def reference(x_prompt, x_sample, cache_kv, page_table, state_win, state_conv,
              c_prompt, c_sample, w_ada, b_ada, norm_g,
              conv_w_pw1, conv_b_pw1, conv_w_dw, conv_b_dw, conv_ln_g, conv_ln_b, conv_w_pw2,
              ffn_w_up, ffn_w_down, kv_norm_g, w_kv, w_cmp1, w_cmp2, pe_cmp,
              nsa_w_in, nsa_w_o, final_norm_g):
    slopes = alibi_slopes()

    def trunk(x, c, conv_prev, attn_setup):
        B, T = x.shape[0], x.shape[1]
        mods = (jax.nn.silu(c) @ w_ada + b_ada).reshape(B, N_MOD, D_MODEL)
        conv_new = []
        attn, extra = None, None
        for l in range(DEPTH):
            m = mods[:, 6 * l:6 * l + 6]
            if l == N_A:
                k0 = 6 * DEPTH
                hk = modulate(x, kv_norm_g, mods[:, k0], mods[:, k0 + 1])
                kv_new = (hk @ w_kv).reshape(B, T, 6, N_KV, HEAD_DIM)
                attn, extra = attn_setup(kv_new)
            h = modulate(x, norm_g[l, 0], m[:, 0], m[:, 1])
            if l < N_A:
                y, st = conv_module(h, conv_prev[l], conv_w_pw1[l], conv_b_pw1[l], conv_w_dw[l],
                                    conv_b_dw[l], conv_ln_g[l], conv_ln_b[l], conv_w_pw2[l])
                conv_new.append(st)
            else:
                y = attn(h, nsa_w_in[l - N_A], nsa_w_o[l - N_A])
            x = x + m[:, 2][:, None, :] * y
            h = modulate(x, norm_g[l, 1], m[:, 3], m[:, 4])
            x = x + m[:, 5][:, None, :] * swiglu(h, ffn_w_up[l], ffn_w_down[l])
        k0 = 6 * DEPTH + 2
        y = modulate(x, final_norm_g, mods[:, k0], mods[:, k0 + 1])
        return y, extra, jnp.stack(conv_new)

    def prompt_setup(kv_new):
        B, T = kv_new.shape[0], kv_new.shape[1]
        ctx = build_ctx(kv_new[:, :, :4], w_cmp1, w_cmp2, pe_cmp)
        pad = ((0, 0), (WINDOW, 0), (0, 0), (0, 0))
        kw_pad = jnp.pad(kv_new[:, :, 4], pad)
        vw_pad = jnp.pad(kv_new[:, :, 5], pad)

        def window_fn(s0):
            kwpos = s0 - WINDOW + jnp.arange(WINDOW + Q_BLK, dtype=jnp.int32)
            return (lax.dynamic_slice_in_dim(kw_pad, s0, WINDOW + Q_BLK, axis=1),
                    lax.dynamic_slice_in_dim(vw_pad, s0, WINDOW + Q_BLK, axis=1), kwpos)

        def attn(h, w_in, w_o):
            return nsa_mixer(h, w_in, w_o, ctx, Q_BLK, 0, window_fn, slopes)

        kv_pages = kv_new[:, :, :4].reshape(B * T // PAGE_SIZE, PAGE_SIZE, 4, N_KV, HEAD_DIM)
        win = kv_new[:, T - min(WINDOW, T):, 4:6]
        return attn, (kv_pages, win)

    def sample_setup(kv_new):
        Bd, Tn = kv_new.shape[0], kv_new.shape[1]
        past = cache_kv[page_table].reshape(Bd, -1, 4, N_KV, HEAD_DIM)
        full = jnp.concatenate([past, kv_new[:, :, :4].astype(past.dtype)], axis=1)
        ctx = build_ctx(full, w_cmp1, w_cmp2, pe_cmp)
        win_all = jnp.concatenate([state_win, kv_new[:, :, 4:6].astype(state_win.dtype)], axis=1)
        n_buf = state_win.shape[1]
        kwpos = PAST_LEN - n_buf + jnp.arange(n_buf + Tn, dtype=jnp.int32)

        def window_fn(s0):
            return win_all[:, :, 0], win_all[:, :, 1], kwpos

        def attn(h, w_in, w_o):
            return nsa_mixer(h, w_in, w_o, ctx, 1, PAST_LEN, window_fn, slopes)

        n_keep = min(WINDOW, PAST_LEN + Tn)
        return attn, (kv_new[:, :, :4], win_all[:, win_all.shape[1] - n_keep:])

    conv0 = jnp.zeros((N_A, x_prompt.shape[0], CONV_W - 1, D_MODEL), x_prompt.dtype)
    y_prompt, (kv_prompt, win_prompt), conv_prompt = trunk(x_prompt, c_prompt, conv0, prompt_setup)
    y_sample, (kv_sample, win_sample), conv_sample = trunk(x_sample, c_sample, state_conv, sample_setup)
    return (y_prompt, y_sample, kv_prompt, kv_sample, win_prompt, win_sample, conv_prompt, conv_sample)
```

```python
import functools

import jax
import jax.numpy as jnp
from jax import lax
from jax.experimental import pallas as pl
from jax.experimental.pallas import tpu as pltpu

F32 = jnp.float32
BF16 = jnp.bfloat16

N_HEADS = 16
N_KV = 4
GROUP = N_HEADS // N_KV
HEAD_DIM = 64
L_CMP = 32
STRIDE = 16
L_SEL = 64
TOP_N = 16
WINDOW = 512
Q_BLK = 128
PAGE = 128
CONV_W = 31
EPS = 1e-6
FORCE = 1e3
NEG = -1e30

LANES = 128
FF_CHUNK = 256
SEL_TILE = 256
PAGES_PER_STEP = 16
VMEM_LIMIT = 56 * 1024 * 1024


def _cparams(sem):
    return pltpu.CompilerParams(dimension_semantics=sem, vmem_limit_bytes=VMEM_LIMIT)


def _const_spec(shape):
    nd = len(shape)
    return pl.BlockSpec(shape, lambda *_: (0,) * nd, pipeline_mode=pl.Buffered(1))


def _dot(a, b):
    return jnp.dot(a, b, preferred_element_type=F32)


def _dot_nt(a, b):
    return lax.dot_general(a, b, (((1,), (1,)), ((), ())), preferred_element_type=F32)


def _silu(x):
    return x * jax.nn.sigmoid(x)


def _rms_mod(x, g, shift, scale):
    ms = jnp.mean(x * x, axis=-1, keepdims=True)
    y = x * lax.rsqrt(ms + EPS) * g
    return y * (1.0 + scale) + shift


def _mod_getter(m_ref, time_major):
    if time_major:
        return lambda k: m_ref[k][None]
    b = pl.program_id(0)
    return lambda k: m_ref[k, pl.ds(b, 1), :][None]


def _mods_kernel(c_ref, w_ref, b_ref, o_ref):
    c = c_ref[...]
    s = _silu(c).astype(BF16)
    o_ref[0] = _dot(s, w_ref[...].astype(BF16)) + b_ref[0]


def _mods(c_all, w_ada, b_ada, n_mod):
    r, d = c_all.shape
    return pl.pallas_call(
        _mods_kernel,
        grid=(n_mod,),
        in_specs=[pl.BlockSpec((r, d), lambda k: (0, 0)),
                  pl.BlockSpec((d, d), lambda k: (0, k)),
                  pl.BlockSpec((1, 1, d), lambda k: (k, 0, 0))],
        out_specs=pl.BlockSpec((1, r, d), lambda k: (k, 0, 0)),
        out_shape=jax.ShapeDtypeStruct((n_mod, r, d), F32),
        compiler_params=_cparams(("arbitrary",)),
        name="mods",
    )(c_all, w_ada, b_ada.reshape(n_mod, 1, d))


class _Tiling:
    def __init__(self, x_shape, time_major, mods_rows):
        self.time_major = time_major
        if time_major:
            tn, bd, d = x_shape
            bk = min(bd, 32)
            assert bd % bk == 0 and bk % 8 == 0
            self.grid = (bd // bk,)
            self.tile = (tn, bk, d)
            self.rows = tn * bk
            self.x_map = lambda j: (0, j, 0)
            self.mod_block = lambda k: (k, bk, d)
            self.mod_map = lambda kblk: (lambda j: (kblk, j, 0))
            self.sem = ("arbitrary",)
        else:
            b, t, d = x_shape
            tr = min(t, 512)
            assert t % tr == 0
            self.grid = (b, t // tr)
            self.tile = (1, tr, d)
            self.rows = tr
            self.x_map = lambda bi, i: (bi, i, 0)
            self.mod_block = lambda k: (k, 8, d)
            self.mod_map = lambda kblk: (lambda bi, i: (kblk, mods_rows // 8, 0))
            self.sem = ("arbitrary", "arbitrary")
        self.d = d

    def x_spec(self):
        return pl.BlockSpec(self.tile, self.x_map)

    def mod_spec(self, k, kblk):
        return pl.BlockSpec(self.mod_block(k), self.mod_map(kblk))


def _conv_core(y, ln_g, ln_b):
    mu = jnp.mean(y, axis=-1, keepdims=True)
    yc = y - mu
    var = jnp.mean(yc * yc, axis=-1, keepdims=True)
    return _silu(yc * lax.rsqrt(var + EPS) * ln_g + ln_b)


def _glu_rows(h2, w1_ref, b1_ref, d):
    a1 = _dot(h2, w1_ref[:, :d]) + b1_ref[:, :d]
    a2 = _dot(h2, w1_ref[:, d:]) + b1_ref[:, d:]
    return a1 * jax.nn.sigmoid(a2)


def _conv_prompt_kernel(x_ref, m_ref, g_ref, w1_ref, b1_ref, wdw_ref, bdw_ref, lng_ref, lnb_ref, w2_ref,
                        xo_ref, st_ref, s_ref):
    tr, d = x_ref.shape[1], x_ref.shape[2]
    halo = 32
    off = halo - (CONV_W - 1)
    mod = _mod_getter(m_ref, False)

    @pl.when(pl.program_id(1) == 0)
    def _():
        s_ref[0:halo, :] = jnp.zeros((halo, d), F32)

    x = x_ref[...]
    h = _rms_mod(x, g_ref[...][None], mod(0), mod(1))
    s_ref[halo:halo + tr, :] = _glu_rows(h.reshape(tr, d).astype(BF16), w1_ref, b1_ref, d)
    acc = jnp.broadcast_to(bdw_ref[...], (tr, d))
    for w in range(CONV_W):
        acc = acc + wdw_ref[w:w + 1, :] * s_ref[pl.ds(off + w, tr), :]
    z = _conv_core(acc, lng_ref[...], lnb_ref[...]).astype(BF16)
    y2 = _dot(z, w2_ref[...])
    xo_ref[...] = x + mod(2) * y2[None]
    tail = s_ref[tr:tr + halo, :]
    st_ref[0] = tail
    s_ref[0:halo, :] = tail


def _conv_sample_kernel(x_ref, p_ref, m_ref, g_ref, w1_ref, b1_ref, wdw_ref, bdw_ref, lng_ref, lnb_ref, w2_ref,
                        xo_ref, st_ref, s_ref):
    tn, bk, d = x_ref.shape
    npv = CONV_W - 1
    mod = _mod_getter(m_ref, True)
    x = x_ref[...]
    h = _rms_mod(x, g_ref[...][None], mod(0), mod(1))
    u = _glu_rows(h.reshape(tn * bk, d).astype(BF16), w1_ref, b1_ref, d).reshape(tn, bk, d)
    s_ref[0:npv] = p_ref[...]
    s_ref[npv:npv + tn] = u
    acc = jnp.broadcast_to(bdw_ref[...][None], (tn, bk, d))
    for w in range(CONV_W):
        acc = acc + wdw_ref[w:w + 1, :][None] * s_ref[w:w + tn]
    z = _conv_core(acc, lng_ref[...][None], lnb_ref[...][None]).astype(BF16)
    y2 = _dot(z.reshape(tn * bk, d), w2_ref[...]).reshape(tn, bk, d)
    xo_ref[...] = x + mod(2) * y2
    st_ref[...] = s_ref[tn:tn + npv]


def _conv_layer(x, prev_tm, mods, mods_rows, l, time_major, g1, w1, b1, wdw, bdw, lng, lnb, w2):
    tl = _Tiling(x.shape, time_major, mods_rows)
    d = tl.d
    weights = [g1.reshape(1, d), w1, b1.reshape(1, 2 * d), wdw, bdw.reshape(1, d), lng.reshape(1, d),
               lnb.reshape(1, d), w2]
    w_specs = [_const_spec(w.shape) for w in weights]
    if time_major:
        tn, bd, _ = x.shape
        bk = tl.tile[1]
        npv = CONV_W - 1
        return pl.pallas_call(
            _conv_sample_kernel,
            grid=tl.grid,
            in_specs=[tl.x_spec(), pl.BlockSpec((npv, bk, d), lambda j: (0, j, 0)), tl.mod_spec(6, l)] + w_specs,
            out_specs=[tl.x_spec(), pl.BlockSpec((npv, bk, d), lambda j: (0, j, 0))],
            out_shape=[jax.ShapeDtypeStruct(x.shape, F32), jax.ShapeDtypeStruct((npv, bd, d), F32)],
            scratch_shapes=[pltpu.VMEM((npv + tn, bk, d), F32)],
            compiler_params=_cparams(tl.sem),
            name="conv_sample",
        )(x, prev_tm, mods, *weights)
    b, t, _ = x.shape
    tr = tl.rows
    return pl.pallas_call(
        _conv_prompt_kernel,
        grid=tl.grid,
        in_specs=[tl.x_spec(), tl.mod_spec(6, l)] + w_specs,
        out_specs=[tl.x_spec(), pl.BlockSpec((1, 32, d), lambda bi, i: (bi, 0, 0))],
        out_shape=[jax.ShapeDtypeStruct(x.shape, F32), jax.ShapeDtypeStruct((b, 32, d), F32)],
        scratch_shapes=[pltpu.VMEM((tr + 32, d), F32)],
        compiler_params=_cparams(tl.sem),
        name="conv_prompt",
    )(x, mods, *weights)


def _ffn_kernel(*refs, time_major, has_attn, final, head_major_o):
    it = iter(refs)
    x_ref = next(it)
    m_ref = next(it)
    g2_ref = next(it)
    if has_attn:
        o_ref = next(it)
        wo_ref = next(it)
    wup_ref = next(it)
    wdn_ref = next(it)
    if final:
        fm_ref = next(it)
        fg_ref = next(it)
    out_ref = next(it)
    hb_ref = next(it)
    acc_ref = next(it)

    a, bk, d = x_ref.shape
    rows = a * bk
    mod = _mod_getter(m_ref, time_major)
    x = x_ref[...]
    if has_attn:
        if head_major_o:
            ocat = jnp.concatenate([o_ref[0, h] for h in range(N_HEADS)], axis=1)
        else:
            ocat = o_ref[...].reshape(rows, o_ref.shape[-1]).astype(BF16)
        x = x + mod(2) * _dot(ocat, wo_ref[...]).reshape(a, bk, d)
    h = _rms_mod(x, g2_ref[...][None], mod(3), mod(4))
    hb_ref[...] = h.reshape(rows, d).astype(BF16)
    acc_ref[...] = jnp.zeros((rows, d), F32)
    c = wdn_ref.shape[1]

    def body(j, carry):
        t = _dot(hb_ref[...], wup_ref[j])
        act = (_silu(t[:, :c]) * t[:, c:]).astype(BF16)
        acc_ref[...] += _dot(act, wdn_ref[j])
        return carry

    lax.fori_loop(0, wup_ref.shape[0], body, 0)
    x = x + mod(5) * acc_ref[...].reshape(a, bk, d)
    if final:
        fmod = _mod_getter(fm_ref, time_major)
        x = _rms_mod(x, fg_ref[...][None], fmod(0), fmod(1))
    out_ref[...] = x


def _ffn_layer(x, mods, mods_rows, l, time_major, g2, wup_c, wdn_c, o=None, wo=None, final_g=None, n_mod_final=None):
    tl = _Tiling(x.shape, time_major, mods_rows)
    d = tl.d
    has_attn = o is not None
    final = final_g is not None
    args = [x, mods, g2.reshape(1, d)]
    specs = [tl.x_spec(), tl.mod_spec(6, l), _const_spec((1, d))]
    head_major_o = False
    if has_attn:
        if time_major:
            specs.append(pl.BlockSpec(tl.tile[:2] + (o.shape[-1],), tl.x_map))
        else:
            head_major_o = True
            tr = tl.rows
            specs.append(pl.BlockSpec((1, N_HEADS, tr, LANES), lambda bi, i: (bi, 0, i, 0)))
        args += [o, wo]
        specs.append(_const_spec(wo.shape))
    args += [wup_c, wdn_c]
    specs += [_const_spec(wup_c.shape), _const_spec(wdn_c.shape)]
    if final:
        args += [mods, final_g.reshape(1, d)]
        specs += [tl.mod_spec(2, n_mod_final), _const_spec((1, d))]
    kern = functools.partial(_ffn_kernel, time_major=time_major, has_attn=has_attn, final=final,
                             head_major_o=head_major_o)
    return pl.pallas_call(
        kern,
        grid=tl.grid,
        in_specs=specs,
        out_specs=tl.x_spec(),
        out_shape=jax.ShapeDtypeStruct(x.shape, F32),
        scratch_shapes=[pltpu.VMEM((tl.rows, d), BF16), pltpu.VMEM((tl.rows, d), F32)],
        compiler_params=_cparams(tl.sem),
        name="ffn_sample" if time_major else "ffn_prompt",
    )(*args)


def _kv_prompt_kernel(x_ref, m_ref, g_ref, wn_ref, wp_ref, kv4_ref, win_ref, ks_ref, kw_ref, vv_ref):
    tr, d = x_ref.shape[1], x_ref.shape[2]
    mod = _mod_getter(m_ref, False)
    hk = _rms_mod(x_ref[...], g_ref[...][None], mod(0), mod(1)).reshape(tr, d).astype(BF16)
    kv = _dot(hk, wn_ref[...])
    n4 = kv4_ref.shape[-1]
    kv4_ref[0] = kv[:, :n4]
    win_ref[0] = kv[:, n4:]
    att = _dot(hk, wp_ref[...])
    t0 = pl.program_id(1) * tr
    blk = (t0 + lax.broadcasted_iota(jnp.int32, (tr, LANES), 0)) // L_SEL
    lane = lax.broadcasted_iota(jnp.int32, (tr, LANES), 1)
    onehot = lane - HEAD_DIM == blk
    for k in range(N_KV):
        ks = att[:, k * LANES:(k + 1) * LANES]
        ks_ref[0, k] = jnp.where(onehot, 1.0, ks).astype(BF16)
        kw_ref[0, k] = att[:, (N_KV + k) * LANES:(N_KV + k + 1) * LANES].astype(BF16)
        vv_ref[0, k] = att[:, (2 * N_KV + k) * LANES:(2 * N_KV + k + 1) * LANES].astype(BF16)


def _kv_sample_kernel(x_ref, m_ref, g_ref, wn_ref, kv_ref):
    tn, bk, d = x_ref.shape
    mod = _mod_getter(m_ref, True)
    hk = _rms_mod(x_ref[...], g_ref[...][None], mod(0), mod(1)).reshape(tn * bk, d).astype(BF16)
    kv_ref[...] = _dot(hk, wn_ref[...]).reshape(tn, bk, kv_ref.shape[-1])


def _kv_proj(x, mods, mods_rows, kblk, time_major, g, w_nat, w_perm):
    tl = _Tiling(x.shape, time_major, mods_rows)
    d = tl.d
    nkv = w_nat.shape[1]
    if time_major:
        return pl.pallas_call(
            _kv_sample_kernel,
            grid=tl.grid,
            in_specs=[tl.x_spec(), tl.mod_spec(2, kblk), _const_spec((1, d)), _const_spec(w_nat.shape)],
            out_specs=pl.BlockSpec(tl.tile[:2] + (nkv,), tl.x_map),
            out_shape=jax.ShapeDtypeStruct(x.shape[:2] + (nkv,), F32),
            compiler_params=_cparams(tl.sem),
            name="kv_sample",
        )(x, mods, g.reshape(1, d), w_nat)
    b, t, _ = x.shape
    tr = tl.rows
    n4 = 4 * N_KV * HEAD_DIM
    hm = pl.BlockSpec((1, N_KV, tr, LANES), lambda bi, i: (bi, 0, i, 0))
    hm_shape = jax.ShapeDtypeStruct((b, N_KV, t, LANES), BF16)
    return pl.pallas_call(
        _kv_prompt_kernel,
        grid=tl.grid,
        in_specs=[tl.x_spec(), tl.mod_spec(2, kblk), _const_spec((1, d)), _const_spec(w_nat.shape),
                  _const_spec(w_perm.shape)],
        out_specs=[pl.BlockSpec((1, tr, n4), tl.x_map), pl.BlockSpec((1, tr, nkv - n4), tl.x_map), hm, hm, hm],
        out_shape=[jax.ShapeDtypeStruct((b, t, n4), F32), jax.ShapeDtypeStruct((b, t, nkv - n4), F32),
                   hm_shape, hm_shape, hm_shape],
        compiler_params=_cparams(tl.sem),
        name="kv_prompt",
    )(x, mods, g.reshape(1, d), w_nat, w_perm)


def _compress_kernel(pt_ref, *refs):
    del pt_ref
    pages = refs[:PAGES_PER_STEP]
    w1_ref, w1f_ref, pe_ref, w2p_ref, w2n_ref = refs[PAGES_PER_STEP:PAGES_PER_STEP + 5]
    kcp_ref, vcp_ref, kcn_ref, vcn_ref = refs[PAGES_PER_STEP + 5:PAGES_PER_STEP + 9]
    x_ref, hs_ref = refs[PAGES_PER_STEP + 9:]
    nh = x_ref.shape[1]
    hid = w2p_ref.shape[1]
    per_page = PAGE // STRIDE
    pg0 = pl.program_id(1) * PAGES_PER_STEP

    pi = lax.broadcasted_iota(jnp.int32, (PAGE, PAGE), 0)
    pj = lax.broadcasted_iota(jnp.int32, (PAGE, PAGE), 1)
    perm = jnp.where(pj == (pi % per_page) * STRIDE + pi // per_page, 1.0, 0.0).astype(BF16)
    for pg in range(PAGES_PER_STEP):
        r0 = pl.multiple_of((pg0 + pg) * per_page, per_page)
        rp = _dot(perm, pages[pg][0].astype(BF16))
        for l in range(STRIDE):
            x_ref[l, pl.ds(r0, per_page), :] = rp[l * per_page:(l + 1) * per_page, :]

    @pl.when(pl.program_id(1) == pl.num_programs(1) - 1)
    def _():
        rows = lax.broadcasted_iota(jnp.int32, (nh, hid), 0)
        hs_ref[nh:nh + 8, :] = jnp.zeros((8, hid), F32)
        for c, (pad_ref, nat_ref) in enumerate(((kcp_ref, kcn_ref), (vcp_ref, vcn_ref))):
            cvec = _dot(pe_ref[c], w1f_ref[c])[0:1, :]
            nat = jnp.zeros((nh, N_KV * HEAD_DIM), F32)
            for pr in range(N_KV // 2):
                lo = c * N_KV * HEAD_DIM + pr * LANES
                acc = jnp.zeros((nh, 4 * hid), F32)
                for l in range(STRIDE):
                    acc = acc + _dot(x_ref[l, :, lo:lo + LANES].astype(BF16), w1_ref[c, l])
                for e in range(2):
                    k = 2 * pr + e
                    hs_ref[0:nh, :] = acc[:, (2 * e + 1) * hid:(2 * e + 2) * hid]
                    hsum = acc[:, 2 * e * hid:(2 * e + 1) * hid] + hs_ref[pl.ds(1, nh), :] + cvec
                    hb = jnp.where(rows < nh - 1, _silu(hsum), 0.0).astype(BF16)
                    pad_ref[0, k] = _dot(hb, w2p_ref[c]).astype(BF16)
                    nat = nat + _dot(hb, w2n_ref[c, k])
            nat_ref[0] = nat.astype(BF16)


def _compress(pages, page_table, cw):
    bs, p = page_table.shape
    assert p % PAGES_PER_STEP == 0
    ng = p // PAGES_PER_STEP
    nh = p * (PAGE // STRIDE)
    wcols = 2 * N_KV * HEAD_DIM
    hid = cw["w2p"].shape[1]

    def page_spec(pg):
        return pl.BlockSpec((1, PAGE, wcols), lambda b, g, pt: (pt[b, g * PAGES_PER_STEP + pg], 0, 0))

    def full(a):
        nd = a.ndim
        return pl.BlockSpec(a.shape, lambda b, g, pt: (0,) * nd)

    consts = [cw["w1"], cw["w1f"], cw["pe"], cw["w2p"], cw["w2n"]]
    pad_shape = jax.ShapeDtypeStruct((bs, N_KV, nh, LANES), BF16)
    nat_shape = jax.ShapeDtypeStruct((bs, nh, N_KV * HEAD_DIM), BF16)
    pad_spec = pl.BlockSpec((1, N_KV, nh, LANES), lambda b, g, pt: (b, 0, 0, 0))
    nat_spec = pl.BlockSpec((1, nh, N_KV * HEAD_DIM), lambda b, g, pt: (b, 0, 0))
    grid_spec = pltpu.PrefetchScalarGridSpec(
        num_scalar_prefetch=1,
        grid=(bs, ng),
        in_specs=[page_spec(pg) for pg in range(PAGES_PER_STEP)] + [full(a) for a in consts],
        out_specs=[pad_spec, pad_spec, nat_spec, nat_spec],
        scratch_shapes=[pltpu.VMEM((STRIDE, nh, wcols), F32), pltpu.VMEM((nh + 8, hid), F32)],
    )
    return pl.pallas_call(
        _compress_kernel,
        grid_spec=grid_spec,
        out_shape=[pad_shape, pad_shape, nat_shape, nat_shape],
        compiler_params=_cparams(("arbitrary", "arbitrary")),
        name="compress",
    )(page_table, *([pages] * PAGES_PER_STEP), *consts)


def _qin_prompt_kernel(x_ref, m_ref, g_ref, w_ref, q_ref, gate_ref):
    tr, d = x_ref.shape[1], x_ref.shape[2]
    mod = _mod_getter(m_ref, False)
    h = _rms_mod(x_ref[...], g_ref[...][None], mod(0), mod(1)).reshape(tr, d).astype(BF16)
    a = _dot(h, w_ref[...])
    for hd in range(N_HEADS):
        q_ref[0, hd] = a[:, hd * LANES:(hd + 1) * LANES].astype(BF16)
    gate_ref[0] = a[:, N_HEADS * LANES:]


def _qin_sample_kernel(x_ref, m_ref, g_ref, w_ref, a_ref):
    tn, bk, d = x_ref.shape
    mod = _mod_getter(m_ref, True)
    h = _rms_mod(x_ref[...], g_ref[...][None], mod(0), mod(1)).reshape(tn * bk, d).astype(BF16)
    a_ref[...] = _dot(h, w_ref[...]).reshape(tn, bk, a_ref.shape[-1])


def _qin(x, mods, mods_rows, l, time_major, g1, w):
    tl = _Tiling(x.shape, time_major, mods_rows)
    d = tl.d
    n = w.shape[1]
    common = dict(grid=tl.grid, compiler_params=_cparams(tl.sem))
    in_specs = [tl.x_spec(), tl.mod_spec(6, l), _const_spec((1, d)), _const_spec(w.shape)]
    if time_major:
        return pl.pallas_call(
            _qin_sample_kernel, in_specs=in_specs,
            out_specs=pl.BlockSpec(tl.tile[:2] + (n,), tl.x_map),
            out_shape=jax.ShapeDtypeStruct(x.shape[:2] + (n,), F32),
            name="qin_sample", **common)(x, mods, g1.reshape(1, d), w)
    b, t, _ = x.shape
    tr = tl.rows
    return pl.pallas_call(
        _qin_prompt_kernel, in_specs=in_specs,
        out_specs=[pl.BlockSpec((1, N_HEADS, tr, LANES), lambda bi, i: (bi, 0, i, 0)),
                   pl.BlockSpec((1, tr, LANES), tl.x_map)],
        out_shape=[jax.ShapeDtypeStruct((b, N_HEADS, t, LANES), BF16), jax.ShapeDtypeStruct((b, t, LANES), F32)],
        name="qin_prompt", **common)(x, mods, g1.reshape(1, d), w)


def _block_rank(score_ref, ns):
    score = score_ref[...]
    blk = lax.broadcasted_iota(jnp.int32, score.shape, 0)
    rank = jnp.zeros(score.shape, F32)
    for i in range(ns):
        si = score_ref[i:i + 1, :]
        rank = rank + jnp.where(blk > i, jnp.where(si >= score, 1.0, 0.0), jnp.where(si > score, 1.0, 0.0))
    return rank


def _online_update(s, v, m, l, acc):
    m_new = jnp.maximum(m, jnp.max(s, axis=1, keepdims=True))
    alpha = jnp.exp(m - m_new)
    p = jnp.exp(s - m_new)
    l = alpha * l + jnp.sum(p, axis=1, keepdims=True)
    acc = alpha * acc + _dot(p.astype(BF16), v)
    return m_new, l, acc


def _attn_prompt_kernel(slopes_ref, q_ref, gate_ref, ks_ref, kw_ref, vv_ref, kc_ref, vc_ref, msel_ref, o_ref,
                        sc_ref, mt_ref, oc_ref):
    qb = q_ref.shape[2]
    t = ks_ref.shape[2]
    nc = kc_ref.shape[2]
    ns = msel_ref.shape[0]
    i = pl.program_id(1)
    s0 = i * qb
    qpos = s0 + lax.broadcasted_iota(jnp.int32, (qb, 1), 0)
    lane = lax.broadcasted_iota(jnp.int32, (qb, LANES), 1)
    low = lane < HEAD_DIM
    gates = jax.nn.sigmoid(gate_ref[0])

    cidx = lax.broadcasted_iota(jnp.int32, (1, nc), 1)
    cend = cidx * STRIDE + (L_CMP - 1)
    cvalid = (cend <= qpos) & (cidx < nc - 1)
    cend_f = cend.astype(F32)

    blk = lax.broadcasted_iota(jnp.int32, (ns, qb), 0)
    qpos_t = s0 + lax.broadcasted_iota(jnp.int32, (ns, qb), 1)
    cur = qpos_t // L_SEL
    valid_t = blk * L_SEL <= qpos_t
    forced = (blk == 0) | (blk == cur) | (blk == cur - 1)

    n_sel_tiles = (s0 + qb + SEL_TILE - 1) // SEL_TILE
    n_win_tiles = jnp.minimum(i, WINDOW // qb) + 1

    def gate_col(col):
        return jnp.sum(jnp.where(lane == col, gates, 0.0), axis=1, keepdims=True)

    def per_kv(kvh, carry):
        imp_t = jnp.zeros((ns, qb), F32)
        for g in range(GROUP):
            hd = kvh * GROUP + g
            sl = slopes_ref[hd]
            sc = _dot_nt(q_ref[0, hd], kc_ref[0, kvh]) + sl * cend_f
            m = jnp.max(jnp.where(cvalid, sc, NEG), axis=1, keepdims=True)
            e = jnp.where(cvalid, jnp.exp(sc - m), 0.0)
            lsum = jnp.sum(e, axis=1, keepdims=True)
            pb = jnp.where(lsum > 0.0, e / lsum, 0.0).astype(BF16)
            oc_ref[g] = _dot(pb, vc_ref[0, kvh])
            imp_t = imp_t + _dot_nt(msel_ref[...], pb)

        sc_ref[...] = jnp.where(valid_t, imp_t + jnp.where(forced, FORCE, 0.0), -jnp.inf)
        keep = (_block_rank(sc_ref, ns) < TOP_N) & valid_t
        mt_ref[...] = jnp.zeros(mt_ref.shape, F32)
        mt_ref[HEAD_DIM:HEAD_DIM + ns, :] = jnp.where(keep, 0.0, NEG)
        mask_b = mt_ref[...].T.astype(BF16)

        for g in range(GROUP):
            hd = kvh * GROUP + g
            sl = slopes_ref[hd]
            qg = q_ref[0, hd]
            qa = jnp.where(low, qg, mask_b)

            def sel_tile(kt, st, masked):
                k0 = pl.multiple_of(kt * SEL_TILE, SEL_TILE)
                kpos = k0 + lax.broadcasted_iota(jnp.int32, (1, SEL_TILE), 1)
                s = _dot_nt(qa, ks_ref[0, kvh, pl.ds(k0, SEL_TILE), :]) + sl * kpos.astype(F32)
                if masked:
                    s = jnp.where(kpos <= qpos, s, NEG)
                return _online_update(s, vv_ref[0, kvh, pl.ds(k0, SEL_TILE), :], *st)

            st = (jnp.full((qb, 1), NEG, F32), jnp.zeros((qb, 1), F32), jnp.zeros((qb, LANES), F32))
            st = lax.fori_loop(0, n_sel_tiles - 1, lambda kt, c: sel_tile(kt, c, False), st)
            _, l_s, acc_s = sel_tile(n_sel_tiles - 1, st, True)

            def win_tile(j, st):
                k0 = pl.multiple_of(s0 - j * qb, qb)
                kpos = k0 + lax.broadcasted_iota(jnp.int32, (1, qb), 1)
                dist = qpos - kpos
                s = _dot_nt(qg, kw_ref[0, kvh, pl.ds(k0, qb), :]) + sl * kpos.astype(F32)
                s = jnp.where((dist >= 0) & (dist < WINDOW), s, NEG)
                return _online_update(s, vv_ref[0, kvh, pl.ds(k0, qb), :], *st)

            st = (jnp.full((qb, 1), NEG, F32), jnp.zeros((qb, 1), F32), jnp.zeros((qb, LANES), F32))
            _, l_w, acc_w = lax.fori_loop(0, n_win_tiles, win_tile, st)

            g_c = gate_col(hd)
            g_s = gate_col(N_HEADS + hd)
            g_w = gate_col(2 * N_HEADS + hd)
            out = jnp.where(low, g_c * oc_ref[g] + g_s * (acc_s / l_s), g_w * (acc_w / l_w))
            o_ref[0, hd] = out.astype(BF16)
        return carry

    lax.fori_loop(0, N_KV, per_kv, 0)


def _attn_prompt(slopes, q, gates, ks, kw, vv, kcp, vcp, msel_t):
    b, _, t, _ = q.shape
    nc = kcp.shape[2]
    ns = msel_t.shape[0]
    assert t % SEL_TILE == 0 and ns <= HEAD_DIM and ns % 8 == 0
    qspec = pl.BlockSpec((1, N_HEADS, Q_BLK, LANES), lambda bi, i: (bi, 0, i, 0))
    kvspec = pl.BlockSpec((1, N_KV, t, LANES), lambda bi, i: (bi, 0, 0, 0))
    cspec = pl.BlockSpec((1, N_KV, nc, LANES), lambda bi, i: (bi, 0, 0, 0))
    return pl.pallas_call(
        _attn_prompt_kernel,
        grid=(b, t // Q_BLK),
        in_specs=[pl.BlockSpec(memory_space=pltpu.SMEM), qspec,
                  pl.BlockSpec((1, Q_BLK, LANES), lambda bi, i: (bi, i, 0)),
                  kvspec, kvspec, kvspec, cspec, cspec,
                  pl.BlockSpec(msel_t.shape, lambda bi, i: (0, 0))],
        out_specs=qspec,
        out_shape=jax.ShapeDtypeStruct(q.shape, BF16),
        scratch_shapes=[pltpu.VMEM((ns, Q_BLK), F32), pltpu.VMEM((LANES, Q_BLK), F32),
                        pltpu.VMEM((GROUP, Q_BLK, LANES), F32)],
        compiler_params=_cparams(("arbitrary", "arbitrary")),
        name="attn_prompt",
    )(slopes, q, gates, ks, kw, vv, kcp, vcp, msel_t)


def _attn_sample_kernel(pt_ref, *refs, past):
    del pt_ref
    npg = past // PAGE
    pages = refs[:npg]
    (q_ref, gate_ref, kc_ref, vc_ref, sw_ref, kvn_ref, slope_ref, msel_ref, o_ref,
     s_ref, sc_ref, st_ref) = refs[npg:]
    tn = q_ref.shape[1]
    hw = N_KV * HEAD_DIM
    rows = N_KV * GROUP * tn
    nc = kc_ref.shape[1]
    ns_pad = sc_ref.shape[0]
    nblk = npg + 1
    n_s = (past + tn + L_SEL - 1) // L_SEL
    f_sel = 2 * hw
    slope = slope_ref[...]

    rr = lax.broadcasted_iota(jnp.int32, (rows, 1), 0)
    tq = rr % tn
    qpos = past + tq
    col = lax.broadcasted_iota(jnp.int32, (tn, hw), 1) // HEAD_DIM
    colr = lax.broadcasted_iota(jnp.int32, (rows, hw), 1) // HEAD_DIM
    rowk = lax.broadcasted_iota(jnp.int32, (rows, hw), 0) // (GROUP * tn)
    diag = colr == rowk

    qs = q_ref[0]
    pieces = []
    for k in range(N_KV):
        for g in range(GROUP):
            pieces.append(jnp.where(col == k, qs[:, g * hw:(g + 1) * hw], 0.0))
    qbd = jnp.concatenate(pieces, axis=0).astype(BF16)

    def softmax_rows(s, valid):
        m = jnp.max(jnp.where(valid, s, NEG), axis=1, keepdims=True)
        e = jnp.where(valid, jnp.exp(s - m), 0.0)
        lsum = jnp.sum(e, axis=1, keepdims=True)
        return jnp.where(lsum > 0.0, e / lsum, 0.0)

    cidx = lax.broadcasted_iota(jnp.int32, (1, nc), 1)
    cend = cidx * STRIDE + (L_CMP - 1)
    cvalid = (cend <= qpos) & (cidx < nc - 1)
    sc = _dot_nt(qbd, kc_ref[0]) + slope * cend.astype(F32)
    pc = softmax_rows(sc, cvalid).astype(BF16)
    o_c = _dot(pc, vc_ref[0])

    imp = _dot_nt(msel_ref[...], pc)
    tot = imp
    for g in range(1, GROUP):
        tot = tot + pltpu.roll(imp, rows - g * tn, 1)
    blk = lax.broadcasted_iota(jnp.int32, (ns_pad, rows), 0)
    lane_r = lax.broadcasted_iota(jnp.int32, (ns_pad, rows), 1)
    qpos_l = past + lane_r % tn
    cur = qpos_l // L_SEL
    valid_l = (blk * L_SEL <= qpos_l) & (blk < n_s)
    forced = (blk == 0) | (blk == cur) | (blk == cur - 1)
    sc_ref[...] = jnp.where(valid_l, tot + jnp.where(forced, FORCE, 0.0), -jnp.inf)
    keep = (_block_rank(sc_ref, n_s) < TOP_N) & valid_l
    lead = (lane_r // tn) % GROUP == 0
    keep_f = jnp.where(keep & lead, 1.0, 0.0)
    spread = keep_f
    for g in range(1, GROUP):
        spread = spread + pltpu.roll(keep_f, g * tn, 1)
    st_ref[...] = jnp.zeros(st_ref.shape, F32)
    st_ref[0:ns_pad, :] = spread
    keep_rows = st_ref[...].T.astype(BF16)

    def key_block(j):
        if j < npg:
            kb = pages[j][0, :, 0:hw]
            vb = pages[j][0, :, hw:2 * hw]
        else:
            pad = jnp.zeros((PAGE - tn, hw), F32)
            kb = jnp.concatenate([kvn_ref[0, :, f_sel:f_sel + hw], pad], axis=0)
            vb = jnp.concatenate([kvn_ref[0, :, f_sel + hw:f_sel + 2 * hw], pad], axis=0)
        return kb.astype(BF16), vb.astype(BF16)

    bsel = lax.broadcasted_iota(jnp.int32, (LANES, PAGE), 0)
    ksub = lax.broadcasted_iota(jnp.int32, (LANES, PAGE), 1) // L_SEL
    kloc = lax.broadcasted_iota(jnp.int32, (1, PAGE), 1)
    mrun = jnp.full((rows, 1), NEG, F32)
    for j in range(nblk):
        kb, _ = key_block(j)
        kpos = j * PAGE + kloc
        expand = jnp.where(bsel == (PAGE // L_SEL) * j + ksub, 1.0, 0.0).astype(BF16)
        kept = _dot(keep_rows, expand)
        s = _dot_nt(qbd, kb) + slope * kpos.astype(F32)
        s = jnp.where((kept > 0.5) & (kpos <= qpos), s, NEG)
        s_ref[:, j * PAGE:(j + 1) * PAGE] = s
        mrun = jnp.maximum(mrun, jnp.max(s, axis=1, keepdims=True))
    lsum = jnp.zeros((rows, 1), F32)
    o_s = jnp.zeros((rows, hw), F32)
    for j in range(nblk):
        _, vb = key_block(j)
        p = jnp.exp(s_ref[:, j * PAGE:(j + 1) * PAGE] - mrun)
        lsum = lsum + jnp.sum(p, axis=1, keepdims=True)
        o_s = o_s + _dot(p.astype(BF16), vb)
    o_s = o_s / lsum

    nbuf = sw_ref.shape[1]
    f_win = 4 * hw
    kw = sw_ref[0, :, 0:hw].astype(BF16)
    vw = sw_ref[0, :, hw:2 * hw].astype(BF16)
    pad = jnp.zeros((PAGE - tn, hw), F32)
    kwn = jnp.concatenate([kvn_ref[0, :, f_win:f_win + hw], pad], axis=0).astype(BF16)
    vwn = jnp.concatenate([kvn_ref[0, :, f_win + hw:f_win + 2 * hw], pad], axis=0).astype(BF16)
    pos_a = past - nbuf + lax.broadcasted_iota(jnp.int32, (1, nbuf), 1)
    pos_b = past + kloc
    s_a = _dot_nt(qbd, kw) + slope * pos_a.astype(F32)
    s_b = _dot_nt(qbd, kwn) + slope * pos_b.astype(F32)
    da = qpos - pos_a
    db = qpos - pos_b
    va = (da >= 0) & (da < WINDOW) & (pos_a >= 0)
    vb_ok = (db >= 0) & (db < WINDOW)
    s_a = jnp.where(va, s_a, NEG)
    s_b = jnp.where(vb_ok, s_b, NEG)
    m = jnp.maximum(jnp.max(s_a, axis=1, keepdims=True), jnp.max(s_b, axis=1, keepdims=True))
    p_a = jnp.exp(s_a - m)
    p_b = jnp.exp(s_b - m)
    l_w = jnp.sum(p_a, axis=1, keepdims=True) + jnp.sum(p_b, axis=1, keepdims=True)
    o_w = (_dot(p_a.astype(BF16), vw) + _dot(p_b.astype(BF16), vwn)) / l_w

    gt = jax.nn.sigmoid(gate_ref[0])
    tot_o = jnp.where(diag, gt[0] * o_c + gt[1] * o_s + gt[2] * o_w, 0.0)
    for g in range(GROUP):
        acc = jnp.zeros((tn, hw), F32)
        for k in range(N_KV):
            r0 = (k * GROUP + g) * tn
            acc = acc + tot_o[r0:r0 + tn, :]
        o_ref[0, :, g * hw:(g + 1) * hw] = acc


def _attn_sample(page_table, cache_pages, q, gates, kcn, vcn, state_win, kv_new, slope_rows, msel_t, past):
    bd, tn, qw = q.shape
    npg = past // PAGE
    hw = N_KV * HEAD_DIM
    rows = N_KV * GROUP * tn
    assert rows == LANES and msel_t.shape[1] == kcn.shape[1]
    ns_pad = msel_t.shape[0]

    def page_spec(pg):
        return pl.BlockSpec((1, PAGE, 2 * hw), lambda b, pt: (pt[b, pg], 0, 1))

    def per_seq(a):
        nd = a.ndim
        return pl.BlockSpec((1,) + a.shape[1:], lambda b, pt: (b,) + (0,) * (nd - 1))

    def full(a):
        nd = a.ndim
        return pl.BlockSpec(a.shape, lambda b, pt: (0,) * nd)

    grid_spec = pltpu.PrefetchScalarGridSpec(
        num_scalar_prefetch=1,
        grid=(bd,),
        in_specs=[page_spec(pg) for pg in range(npg)]
        + [per_seq(q), per_seq(gates), per_seq(kcn), per_seq(vcn), per_seq(state_win), per_seq(kv_new),
           full(slope_rows), full(msel_t)],
        out_specs=pl.BlockSpec((1, tn, N_HEADS * HEAD_DIM), lambda b, pt: (b, 0, 0)),
        scratch_shapes=[pltpu.VMEM((rows, (npg + 1) * PAGE), F32), pltpu.VMEM((ns_pad, rows), F32),
                        pltpu.VMEM((LANES, rows), F32)],
    )
    return pl.pallas_call(
        functools.partial(_attn_sample_kernel, past=past),
        grid_spec=grid_spec,
        out_shape=jax.ShapeDtypeStruct((bd, tn, N_HEADS * HEAD_DIM), F32),
        compiler_params=_cparams(("arbitrary",)),
        name="attn_sample",
    )(page_table, *([cache_pages] * npg), q, gates, kcn, vcn, state_win, kv_new, slope_rows, msel_t)


def _msel_t(n_c_rows, n_s_rows):
    i0 = jnp.arange(n_c_rows)[None, :] * STRIDE
    j0 = jnp.arange(n_s_rows)[:, None] * L_SEL
    return ((i0 < j0 + L_SEL) & (i0 + L_CMP > j0)).astype(BF16)


def _prep_ffn(w_up, w_down):
    d, two_ff = w_up.shape
    ff = two_ff // 2
    n = ff // FF_CHUNK
    assert n * FF_CHUNK == ff
    a = w_up[:, :ff].reshape(d, n, FF_CHUNK)
    b = w_up[:, ff:].reshape(d, n, FF_CHUNK)
    wup_c = jnp.concatenate([a, b], axis=2).transpose(1, 0, 2).astype(BF16)
    wdn_c = w_down.reshape(n, FF_CHUNK, w_down.shape[1]).astype(BF16)
    return wup_c, wdn_c


def _prep_compress(w_cmp1, w_cmp2, pe_cmp):
    hid = w_cmp1.shape[-1]
    wl = jnp.concatenate([w_cmp1[:, :STRIDE], w_cmp1[:, STRIDE:]], axis=-1)
    z = jnp.zeros_like(wl)
    w1 = jnp.concatenate([jnp.concatenate([wl, z], axis=-1), jnp.concatenate([z, wl], axis=-1)], axis=2)
    w1f = w_cmp1.reshape(2, L_CMP * HEAD_DIM, hid)
    pe = jnp.zeros((2, 8, L_CMP * HEAD_DIM), F32).at[:, 0].set(pe_cmp.reshape(2, -1))
    w2p = jnp.concatenate([w_cmp2, jnp.zeros_like(w_cmp2)], axis=-1)
    w2n = jnp.zeros((2, N_KV, hid, N_KV * HEAD_DIM), F32)
    for k in range(N_KV):
        w2n = w2n.at[:, k, :, k * HEAD_DIM:(k + 1) * HEAD_DIM].set(w_cmp2)
    return dict(w1=w1.astype(BF16), w1f=w1f.astype(BF16), pe=pe.astype(BF16), w2p=w2p.astype(BF16),
                w2n=w2n.astype(BF16))


def _prep_kv_perm(w_kv):
    d = w_kv.shape[0]
    w6 = w_kv.reshape(d, 6, N_KV, HEAD_DIM)
    z = jnp.zeros((d, N_KV, HEAD_DIM), w_kv.dtype)
    ks = jnp.concatenate([w6[:, 2], z], axis=-1)
    kw = jnp.concatenate([w6[:, 4], z], axis=-1)
    vv = jnp.concatenate([w6[:, 3], w6[:, 5]], axis=-1)
    return jnp.concatenate([ks, kw, vv], axis=1).reshape(d, 3 * N_KV * LANES).astype(BF16)


def _prep_attn_prompt(w_in, w_o):
    d = w_in.shape[0]
    qw = N_HEADS * HEAD_DIM
    scale = HEAD_DIM ** -0.5
    wq = (w_in[:, :qw] * scale).reshape(d, N_HEADS, HEAD_DIM)
    wq = jnp.concatenate([wq, jnp.zeros_like(wq)], axis=-1).reshape(d, N_HEADS * LANES)
    wg = w_in[:, qw:].reshape(d, N_HEADS, 3).transpose(0, 2, 1).reshape(d, 3 * N_HEADS)
    wg = jnp.concatenate([wg, jnp.zeros((d, LANES - 3 * N_HEADS), w_in.dtype)], axis=1)
    w_in_p = jnp.concatenate([wq, wg], axis=1).astype(BF16)
    wo3 = w_o.reshape(N_HEADS, HEAD_DIM, w_o.shape[1])
    wo_p = jnp.concatenate([wo3, wo3], axis=1).reshape(N_HEADS * LANES, w_o.shape[1]).astype(BF16)
    return w_in_p, wo_p


def _prep_attn_sample(w_in, w_o):
    d = w_in.shape[0]
    qw = N_HEADS * HEAD_DIM
    scale = HEAD_DIM ** -0.5
    wq = (w_in[:, :qw] * scale).reshape(d, N_KV, GROUP, HEAD_DIM).transpose(0, 2, 1, 3).reshape(d, qw)
    w_in_s = jnp.concatenate([wq, w_in[:, qw:]], axis=1).astype(BF16)
    wo_s = w_o.reshape(N_KV, GROUP, HEAD_DIM, w_o.shape[1]).transpose(1, 0, 2, 3).reshape(qw, w_o.shape[1])
    return w_in_s, wo_s.astype(BF16)


def kernel(x_prompt, x_sample, cache_kv, page_table, state_win, state_conv, c_prompt, c_sample, w_ada, b_ada, norm_g, conv_w_pw1, conv_b_pw1, conv_w_dw, conv_b_dw, conv_ln_g, conv_ln_b, conv_w_pw2, ffn_w_up, ffn_w_down, kv_norm_g, w_kv, w_cmp1, w_cmp2, pe_cmp, nsa_w_in, nsa_w_o, final_norm_g):
    b, t, d = x_prompt.shape
    bd, tn, _ = x_sample.shape
    depth = norm_g.shape[0]
    n_a = conv_w_pw1.shape[0]
    n_mod = b_ada.shape[0] // d
    past = page_table.shape[1] * PAGE
    hw = N_KV * HEAD_DIM
    assert bd % 8 == 0 and b <= 8 and n_mod == 6 * depth + 4

    c_all = jnp.concatenate([c_sample, c_prompt, jnp.zeros((8 - b % 8, d), F32)], axis=0)
    mods = _mods(c_all, w_ada, b_ada, n_mod)

    ffn_w = [_prep_ffn(ffn_w_up[l], ffn_w_down[l]) for l in range(depth)]
    w_pw1 = conv_w_pw1.astype(BF16)
    w_pw2 = conv_w_pw2.astype(BF16)
    w_kv_nat = w_kv.astype(BF16)
    w_kv_perm = _prep_kv_perm(w_kv)
    cw = _prep_compress(w_cmp1, w_cmp2, pe_cmp)
    slopes = jnp.exp2(-8.0 * jnp.arange(1, N_HEADS + 1, dtype=F32) / N_HEADS)
    kblk_kv = (6 * depth) // 2
    kblk_final = (6 * depth + 2) // 2

    def trunk(x, time_major, conv_prev_tm, attn_fn_builder):
        conv_states = []
        attn_fn = None
        extra = None
        for l in range(depth):
            if l == n_a:
                attn_fn, extra = attn_fn_builder(x)
            if l < n_a:
                x, st = _conv_layer(x, None if conv_prev_tm is None else conv_prev_tm[l], mods, bd, l, time_major,
                                    norm_g[l, 0], w_pw1[l], conv_b_pw1[l], conv_w_dw[l], conv_b_dw[l],
                                    conv_ln_g[l], conv_ln_b[l], w_pw2[l])
                conv_states.append(st)
                o = wo = None
            else:
                o, wo = attn_fn(x, l)
            last = l == depth - 1
            x = _ffn_layer(x, mods, bd, l, time_major, norm_g[l, 1], ffn_w[l][0], ffn_w[l][1], o=o, wo=wo,
                           final_g=final_norm_g if last else None, n_mod_final=kblk_final)
        return x, extra, conv_states

    attn_w_p = [_prep_attn_prompt(nsa_w_in[i], nsa_w_o[i]) for i in range(depth - n_a)]

    def prompt_attn_builder(x):
        kv4, win_all, ks, kw, vv = _kv_proj(x, mods, bd, kblk_kv, False, kv_norm_g, w_kv_nat, w_kv_perm)
        kv_pages = kv4.reshape(b * t // PAGE, PAGE, 4 * hw)
        ident = jnp.arange(b * t // PAGE, dtype=jnp.int32).reshape(b, t // PAGE)
        kcp, vcp, _, _ = _compress(kv_pages, ident, cw)
        msel_t = _msel_t(t // STRIDE, t // L_SEL)

        def attn(xl, l):
            w_in_p, wo_p = attn_w_p[l - n_a]
            q, gates = _qin(xl, mods, bd, l, False, norm_g[l, 0], w_in_p)
            return _attn_prompt(slopes, q, gates, ks, kw, vv, kcp, vcp, msel_t), wo_p

        nw = min(WINDOW, t)
        extra = (kv_pages.reshape(b * t // PAGE, PAGE, 4, N_KV, HEAD_DIM),
                 win_all[:, t - nw:].reshape(b, nw, 2, N_KV, HEAD_DIM))
        return attn, extra

    y_prompt, (kv_prompt, win_prompt), conv_p = trunk(x_prompt, False, None, prompt_attn_builder)
    conv_prompt = jnp.stack([st[:, 32 - (CONV_W - 1):] for st in conv_p])

    attn_w_s = [_prep_attn_sample(nsa_w_in[i], nsa_w_o[i]) for i in range(depth - n_a)]
    cache_pages = cache_kv.reshape(cache_kv.shape[0], PAGE, 4 * hw)
    nbuf = state_win.shape[1]
    state_win2 = state_win.reshape(bd, nbuf, 2 * hw)
    slope_rows = jnp.repeat(slopes, tn).reshape(N_HEADS * tn, 1)

    def sample_attn_builder(x):
        kv_tm = _kv_proj(x, mods, bd, kblk_kv, True, kv_norm_g, w_kv_nat, None)
        kv_new = kv_tm.transpose(1, 0, 2)
        _, _, kcn, vcn = _compress(cache_pages, page_table, cw)
        n_s = -(-(past + tn) // L_SEL)
        msel_t = _msel_t(kcn.shape[1], -(-n_s // 8) * 8)

        def attn(xl, l):
            w_in_s, wo_s = attn_w_s[l - n_a]
            a = _qin(xl, mods, bd, l, True, norm_g[l, 0], w_in_s).transpose(1, 0, 2)
            q = a[..., :N_HEADS * HEAD_DIM]
            gates = a[..., N_HEADS * HEAD_DIM:].reshape(bd, tn, N_HEADS, 3).transpose(0, 3, 2, 1)
            gates = gates.reshape(bd, 3, N_HEADS * tn, 1)
            o = _attn_sample(page_table, cache_pages, q, gates, kcn, vcn, state_win2, kv_new, slope_rows, msel_t,
                             past)
            return o.transpose(1, 0, 2), wo_s

        win_all = jnp.concatenate([state_win2, kv_new[..., 4 * hw:]], axis=1)
        n_keep = min(WINDOW, past + tn)
        extra = (kv_new[..., :4 * hw].reshape(bd, tn, 4, N_KV, HEAD_DIM),
                 win_all[:, win_all.shape[1] - n_keep:].reshape(bd, n_keep, 2, N_KV, HEAD_DIM))
        return attn, extra

    conv_prev_tm = state_conv.transpose(0, 2, 1, 3)
    y_s_tm, (kv_sample, win_sample), conv_s = trunk(x_sample.transpose(1, 0, 2), True, conv_prev_tm,
                                                    sample_attn_builder)
    y_sample = y_s_tm.transpose(1, 0, 2)
    conv_sample = jnp.stack(conv_s).transpose(0, 2, 1, 3)

    return (y_prompt, y_sample, kv_prompt, kv_sample, win_prompt, win_sample, conv_prompt, conv_sample)
```

```python
import functools

import jax
import jax.numpy as jnp
from jax import lax
from jax.experimental import pallas as pl
from jax.experimental.pallas import tpu as pltpu

F32 = jnp.float32
BF16 = jnp.bfloat16

N_HEADS = 16
N_KV = 4
GROUP = N_HEADS // N_KV
HEAD_DIM = 64
L_CMP = 32
STRIDE = 16
L_SEL = 64
TOP_N = 16
WINDOW = 512
Q_BLK = 128
PAGE = 128
CONV_W = 31
EPS = 1e-6
FORCE = 1e3
NEG = -1e30

LANES = 128
FF_CHUNK = 256
SEL_TILE = 256
PAGES_PER_STEP = 16
VMEM_LIMIT = 56 * 1024 * 1024


def _cparams(sem):
    return pltpu.CompilerParams(dimension_semantics=sem, vmem_limit_bytes=VMEM_LIMIT)


def _const_spec(shape):
    nd = len(shape)
    return pl.BlockSpec(shape, lambda *_: (0,) * nd, pipeline_mode=pl.Buffered(1))


def _dot(a, b):
    return jnp.dot(a, b, preferred_element_type=F32)


def _dot_nt(a, b):
    return lax.dot_general(a, b, (((1,), (1,)), ((), ())), preferred_element_type=F32)


def _silu(x):
    return x * jax.nn.sigmoid(x)


def _rms_mod(x, g, shift, scale):
    ms = jnp.mean(x * x, axis=-1, keepdims=True)
    y = x * lax.rsqrt(ms + EPS) * g
    return y * (1.0 + scale) + shift


def _mod_getter(m_ref, time_major):
    if time_major:
        return lambda k: m_ref[k][None]
    b = pl.program_id(0)
    return lambda k: m_ref[k, pl.ds(b, 1), :][None]


def _mods_kernel(c_ref, w_ref, b_ref, o_ref):
    c = c_ref[...]
    s = _silu(c).astype(BF16)
    o_ref[0] = _dot(s, w_ref[...].astype(BF16)) + b_ref[0]


def _mods(c_all, w_ada, b_ada, n_mod):
    r, d = c_all.shape
    return pl.pallas_call(
        _mods_kernel,
        grid=(n_mod,),
        in_specs=[pl.BlockSpec((r, d), lambda k: (0, 0)),
                  pl.BlockSpec((d, d), lambda k: (0, k)),
                  pl.BlockSpec((1, 1, d), lambda k: (k, 0, 0))],
        out_specs=pl.BlockSpec((1, r, d), lambda k: (k, 0, 0)),
        out_shape=jax.ShapeDtypeStruct((n_mod, r, d), F32),
        compiler_params=_cparams(("arbitrary",)),
        name="mods",
    )(c_all, w_ada, b_ada.reshape(n_mod, 1, d))


class _Tiling:
    def __init__(self, x_shape, time_major, mods_rows, bk=None):
        self.time_major = time_major
        if time_major:
            tn, bd, d = x_shape
            bk = min(bd, 32) if bk is None else bk
            assert bd % bk == 0 and bk % 8 == 0
            self.grid = (bd // bk,)
            self.tile = (tn, bk, d)
            self.rows = tn * bk
            self.x_map = lambda j: (0, j, 0)
            self.mod_block = lambda k: (k, bk, d)
            self.mod_map = lambda kblk: (lambda j: (kblk, j, 0))
            self.sem = ("arbitrary",)
        else:
            b, t, d = x_shape
            tr = min(t, 512)
            assert t % tr == 0
            self.grid = (b, t // tr)
            self.tile = (1, tr, d)
            self.rows = tr
            self.x_map = lambda bi, i: (bi, i, 0)
            self.mod_block = lambda k: (k, 8, d)
            self.mod_map = lambda kblk: (lambda bi, i: (kblk, mods_rows // 8, 0))
            self.sem = ("arbitrary", "arbitrary")
        self.d = d

    def x_spec(self):
        return pl.BlockSpec(self.tile, self.x_map)

    def mod_spec(self, k, kblk):
        return pl.BlockSpec(self.mod_block(k), self.mod_map(kblk))


def _conv_core(y, ln_g, ln_b):
    mu = jnp.mean(y, axis=-1, keepdims=True)
    yc = y - mu
    var = jnp.mean(yc * yc, axis=-1, keepdims=True)
    return _silu(yc * lax.rsqrt(var + EPS) * ln_g + ln_b)


def _glu_rows(h2, w1_ref, b1_ref, d):
    a1 = _dot(h2, w1_ref[:, :d]) + b1_ref[:, :d]
    a2 = _dot(h2, w1_ref[:, d:]) + b1_ref[:, d:]
    return a1 * jax.nn.sigmoid(a2)


def _conv_prompt_kernel(x_ref, m_ref, g_ref, w1_ref, b1_ref, wdw_ref, bdw_ref, lng_ref, lnb_ref, w2_ref,
                        xo_ref, st_ref, s_ref):
    tr, d = x_ref.shape[1], x_ref.shape[2]
    halo = 32
    off = halo - (CONV_W - 1)
    mod = _mod_getter(m_ref, False)

    @pl.when(pl.program_id(1) == 0)
    def _():
        s_ref[0:halo, :] = jnp.zeros((halo, d), F32)

    x = x_ref[...]
    h = _rms_mod(x, g_ref[...][None], mod(0), mod(1))
    s_ref[halo:halo + tr, :] = _glu_rows(h.reshape(tr, d).astype(BF16), w1_ref, b1_ref, d)
    acc = jnp.broadcast_to(bdw_ref[...], (tr, d))
    for w in range(CONV_W):
        acc = acc + wdw_ref[w:w + 1, :] * s_ref[pl.ds(off + w, tr), :]
    z = _conv_core(acc, lng_ref[...], lnb_ref[...]).astype(BF16)
    y2 = _dot(z, w2_ref[...])
    xo_ref[...] = x + mod(2) * y2[None]
    tail = s_ref[tr:tr + halo, :]
    st_ref[0] = tail
    s_ref[0:halo, :] = tail


def _conv_sample_kernel(x_ref, p_ref, m_ref, g_ref, w1_ref, b1_ref, wdw_ref, bdw_ref, lng_ref, lnb_ref, w2_ref,
                        xo_ref, st_ref, s_ref):
    tn, bk, d = x_ref.shape
    npv = CONV_W - 1
    mod = _mod_getter(m_ref, True)
    x = x_ref[...]
    h = _rms_mod(x, g_ref[...][None], mod(0), mod(1))
    u = _glu_rows(h.reshape(tn * bk, d).astype(BF16), w1_ref, b1_ref, d).reshape(tn, bk, d)
    s_ref[0:npv] = p_ref[...]
    s_ref[npv:npv + tn] = u
    acc = jnp.broadcast_to(bdw_ref[...][None], (tn, bk, d))
    for w in range(CONV_W):
        acc = acc + wdw_ref[w:w + 1, :][None] * s_ref[w:w + tn]
    z = _conv_core(acc, lng_ref[...][None], lnb_ref[...][None]).astype(BF16)
    y2 = _dot(z.reshape(tn * bk, d), w2_ref[...]).reshape(tn, bk, d)
    xo_ref[...] = x + mod(2) * y2
    st_ref[...] = s_ref[tn:tn + npv]


def _conv_layer(x, prev_tm, mods, mods_rows, l, time_major, g1, w1, b1, wdw, bdw, lng, lnb, w2):
    tl = _Tiling(x.shape, time_major, mods_rows)
    d = tl.d
    weights = [g1.reshape(1, d), w1, b1.reshape(1, 2 * d), wdw, bdw.reshape(1, d), lng.reshape(1, d),
               lnb.reshape(1, d), w2]
    w_specs = [_const_spec(w.shape) for w in weights]
    if time_major:
        tn, bd, _ = x.shape
        bk = tl.tile[1]
        npv = CONV_W - 1
        return pl.pallas_call(
            _conv_sample_kernel,
            grid=tl.grid,
            in_specs=[tl.x_spec(), pl.BlockSpec((npv, bk, d), lambda j: (0, j, 0)), tl.mod_spec(6, l)] + w_specs,
            out_specs=[tl.x_spec(), pl.BlockSpec((npv, bk, d), lambda j: (0, j, 0))],
            out_shape=[jax.ShapeDtypeStruct(x.shape, F32), jax.ShapeDtypeStruct((npv, bd, d), F32)],
            scratch_shapes=[pltpu.VMEM((npv + tn, bk, d), F32)],
            compiler_params=_cparams(tl.sem),
            name="conv_sample",
        )(x, prev_tm, mods, *weights)
    b, t, _ = x.shape
    tr = tl.rows
    return pl.pallas_call(
        _conv_prompt_kernel,
        grid=tl.grid,
        in_specs=[tl.x_spec(), tl.mod_spec(6, l)] + w_specs,
        out_specs=[tl.x_spec(), pl.BlockSpec((1, 32, d), lambda bi, i: (bi, 0, 0))],
        out_shape=[jax.ShapeDtypeStruct(x.shape, F32), jax.ShapeDtypeStruct((b, 32, d), F32)],
        scratch_shapes=[pltpu.VMEM((tr + 32, d), F32)],
        compiler_params=_cparams(tl.sem),
        name="conv_prompt",
    )(x, mods, *weights)


def _ffn_kernel(*refs, time_major, has_attn, final, head_major_o):
    it = iter(refs)
    x_ref = next(it)
    m_ref = next(it)
    g2_ref = next(it)
    if has_attn:
        o_ref = next(it)
        wo_ref = next(it)
    wup_ref = next(it)
    wdn_ref = next(it)
    if final:
        fm_ref = next(it)
        fg_ref = next(it)
    out_ref = next(it)
    hb_ref = next(it)
    acc_ref = next(it)

    a, bk, d = x_ref.shape
    rows = a * bk
    mod = _mod_getter(m_ref, time_major)
    x = x_ref[...]
    if has_attn:
        if head_major_o:
            ocat = jnp.concatenate([o_ref[0, h] for h in range(N_HEADS)], axis=1)
        else:
            ocat = o_ref[...].reshape(rows, o_ref.shape[-1]).astype(BF16)
        x = x + mod(2) * _dot(ocat, wo_ref[...]).reshape(a, bk, d)
    h = _rms_mod(x, g2_ref[...][None], mod(3), mod(4))
    hb_ref[...] = h.reshape(rows, d).astype(BF16)
    acc_ref[...] = jnp.zeros((rows, d), F32)
    c = wdn_ref.shape[1]

    def body(j, carry):
        t = _dot(hb_ref[...], wup_ref[j])
        act = (_silu(t[:, :c]) * t[:, c:]).astype(BF16)
        acc_ref[...] += _dot(act, wdn_ref[j])
        return carry

    lax.fori_loop(0, wup_ref.shape[0], body, 0)
    x = x + mod(5) * acc_ref[...].reshape(a, bk, d)
    if final:
        fmod = _mod_getter(fm_ref, time_major)
        x = _rms_mod(x, fg_ref[...][None], fmod(0), fmod(1))
    out_ref[...] = x


def _ffn_layer(x, mods, mods_rows, l, time_major, g2, wup_c, wdn_c, o=None, wo=None, final_g=None, n_mod_final=None):
    tl = _Tiling(x.shape, time_major, mods_rows)
    d = tl.d
    has_attn = o is not None
    final = final_g is not None
    args = [x, mods, g2.reshape(1, d)]
    specs = [tl.x_spec(), tl.mod_spec(6, l), _const_spec((1, d))]
    head_major_o = False
    if has_attn:
        if time_major:
            specs.append(pl.BlockSpec(tl.tile[:2] + (o.shape[-1],), tl.x_map))
        else:
            head_major_o = True
            tr = tl.rows
            specs.append(pl.BlockSpec((1, N_HEADS, tr, LANES), lambda bi, i: (bi, 0, i, 0)))
        args += [o, wo]
        specs.append(_const_spec(wo.shape))
    args += [wup_c, wdn_c]
    specs += [_const_spec(wup_c.shape), _const_spec(wdn_c.shape)]
    if final:
        args += [mods, final_g.reshape(1, d)]
        specs += [tl.mod_spec(2, n_mod_final), _const_spec((1, d))]
    kern = functools.partial(_ffn_kernel, time_major=time_major, has_attn=has_attn, final=final,
                             head_major_o=head_major_o)
    return pl.pallas_call(
        kern,
        grid=tl.grid,
        in_specs=specs,
        out_specs=tl.x_spec(),
        out_shape=jax.ShapeDtypeStruct(x.shape, F32),
        scratch_shapes=[pltpu.VMEM((tl.rows, d), BF16), pltpu.VMEM((tl.rows, d), F32)],
        compiler_params=_cparams(tl.sem),
        name="ffn_sample" if time_major else "ffn_prompt",
    )(*args)


def _kv_prompt_kernel(x_ref, m_ref, g_ref, wt_ref, wp_ref, kvt_ref, wint_ref, ks_ref, kw_ref, vs_ref, vw_ref):
    tr, d = x_ref.shape[1], x_ref.shape[2]
    mod = _mod_getter(m_ref, False)
    hk = _rms_mod(x_ref[...], g_ref[...][None], mod(0), mod(1)).reshape(tr, d).astype(BF16)
    kvt = _dot_nt(wt_ref[...], hk)
    n4 = kvt_ref.shape[1]
    for p in range(tr // PAGE):
        kvt_ref[p] = kvt[:n4, p * PAGE:(p + 1) * PAGE]
    wint_ref[0] = kvt[n4:, :]
    att = _dot(hk, wp_ref[...])
    t0 = pl.program_id(1) * tr
    blk = (t0 + lax.broadcasted_iota(jnp.int32, (tr, LANES), 0)) // L_SEL
    lane = lax.broadcasted_iota(jnp.int32, (tr, LANES), 1)
    onehot = lane - HEAD_DIM == blk
    low = lane < HEAD_DIM
    for k in range(N_KV):
        def part(j):
            return att[:, (j * N_KV + k) * LANES:(j * N_KV + k + 1) * LANES]
        ks_ref[0, k] = jnp.where(onehot, 1.0, part(0)).astype(BF16)
        kw_ref[0, k] = part(1).astype(BF16)
        vs_ref[0, k] = jnp.where(low, part(2), 1.0).astype(BF16)
        vw_ref[0, k] = jnp.where(low, 1.0, part(3)).astype(BF16)


def _kv_sample_kernel(x_ref, m_ref, g_ref, wt_ref, kvt_ref):
    tn, bk, d = x_ref.shape
    mod = _mod_getter(m_ref, True)
    hk = _rms_mod(x_ref[...], g_ref[...][None], mod(0), mod(1)).reshape(tn * bk, d).astype(BF16)
    kvt_ref[...] = _dot_nt(wt_ref[...], hk)


def _kv_proj(x, mods, mods_rows, kblk, time_major, g, w_t, w_perm):
    d = x.shape[-1]
    nkv = w_t.shape[0]
    if time_major:
        tn, bd, _ = x.shape
        tl = _Tiling(x.shape, True, mods_rows, bk=bd)
        return pl.pallas_call(
            _kv_sample_kernel,
            grid=tl.grid,
            in_specs=[tl.x_spec(), tl.mod_spec(2, kblk), _const_spec((1, d)), _const_spec(w_t.shape)],
            out_specs=pl.BlockSpec((nkv, tn * bd), lambda j: (0, 0)),
            out_shape=jax.ShapeDtypeStruct((nkv, tn * bd), F32),
            compiler_params=_cparams(tl.sem),
            name="kv_sample",
        )(x, mods, g.reshape(1, d), w_t)
    tl = _Tiling(x.shape, False, mods_rows)
    b, t, _ = x.shape
    tr = tl.rows
    n4 = 4 * N_KV * HEAD_DIM
    nw = min(WINDOW, t)
    assert tr == nw and tr % PAGE == 0
    ppt = tr // PAGE
    hm = pl.BlockSpec((1, N_KV, tr, LANES), lambda bi, i: (bi, 0, i, 0))
    hm_shape = jax.ShapeDtypeStruct((b, N_KV, t, LANES), BF16)
    return pl.pallas_call(
        _kv_prompt_kernel,
        grid=tl.grid,
        in_specs=[tl.x_spec(), tl.mod_spec(2, kblk), _const_spec((1, d)), _const_spec(w_t.shape),
                  _const_spec(w_perm.shape)],
        out_specs=[pl.BlockSpec((ppt, n4, PAGE), lambda bi, i: (bi * (t // tr) + i, 0, 0)),
                   pl.BlockSpec((1, nkv - n4, nw), lambda bi, i: (bi, 0, 0)), hm, hm, hm, hm],
        out_shape=[jax.ShapeDtypeStruct((b * t // PAGE, n4, PAGE), F32),
                   jax.ShapeDtypeStruct((b, nkv - n4, nw), F32), hm_shape, hm_shape, hm_shape, hm_shape],
        compiler_params=_cparams(tl.sem),
        name="kv_prompt",
    )(x, mods, g.reshape(1, d), w_t, w_perm)


def _compress_kernel(pt_ref, *refs):
    del pt_ref
    pages = refs[:PAGES_PER_STEP]
    w1_ref, w1f_ref, pe_ref, w2p_ref, w2n_ref = refs[PAGES_PER_STEP:PAGES_PER_STEP + 5]
    kcp_ref, vcp_ref, kcn_ref, vcn_ref = refs[PAGES_PER_STEP + 5:PAGES_PER_STEP + 9]
    x_ref, hs_ref = refs[PAGES_PER_STEP + 9:]
    nh = x_ref.shape[1]
    hid = w2p_ref.shape[1]
    per_page = PAGE // STRIDE
    pg0 = pl.program_id(1) * PAGES_PER_STEP

    pi = lax.broadcasted_iota(jnp.int32, (PAGE, PAGE), 0)
    pj = lax.broadcasted_iota(jnp.int32, (PAGE, PAGE), 1)
    perm = jnp.where(pj == (pi % per_page) * STRIDE + pi // per_page, 1.0, 0.0).astype(BF16)
    for pg in range(PAGES_PER_STEP):
        r0 = pl.multiple_of((pg0 + pg) * per_page, per_page)
        rp = _dot_nt(perm, pages[pg][0].astype(BF16))
        for l in range(STRIDE):
            x_ref[l, pl.ds(r0, per_page), :] = rp[l * per_page:(l + 1) * per_page, :]

    @pl.when(pl.program_id(1) == pl.num_programs(1) - 1)
    def _():
        rows = lax.broadcasted_iota(jnp.int32, (nh, hid), 0)
        hs_ref[nh:nh + 8, :] = jnp.zeros((8, hid), F32)
        for c, (pad_ref, nat_ref) in enumerate(((kcp_ref, kcn_ref), (vcp_ref, vcn_ref))):
            cvec = _dot(pe_ref[c], w1f_ref[c])[0:1, :]
            nat = jnp.zeros((nh, N_KV * HEAD_DIM), F32)
            for pr in range(N_KV // 2):
                lo = c * N_KV * HEAD_DIM + pr * LANES
                acc = jnp.zeros((nh, 4 * hid), F32)
                for l in range(STRIDE):
                    acc = acc + _dot(x_ref[l, :, lo:lo + LANES].astype(BF16), w1_ref[c, l])
                for e in range(2):
                    k = 2 * pr + e
                    hs_ref[0:nh, :] = acc[:, (2 * e + 1) * hid:(2 * e + 2) * hid]
                    hsum = acc[:, 2 * e * hid:(2 * e + 1) * hid] + hs_ref[pl.ds(1, nh), :] + cvec
                    hb = jnp.where(rows < nh - 1, _silu(hsum), 0.0).astype(BF16)
                    pad_ref[0, k] = _dot(hb, w2p_ref[c]).astype(BF16)
                    nat = nat + _dot(hb, w2n_ref[c, k])
            nat_ref[0] = nat.astype(BF16)


def _compress(pages, page_table, cw):
    bs, p = page_table.shape
    assert p % PAGES_PER_STEP == 0
    ng = p // PAGES_PER_STEP
    nh = p * (PAGE // STRIDE)
    wcols = 2 * N_KV * HEAD_DIM
    hid = cw["w2p"].shape[1]

    def page_spec(pg):
        return pl.BlockSpec((1, wcols, PAGE), lambda b, g, pt: (pt[b, g * PAGES_PER_STEP + pg], 0, 0))

    def full(a):
        nd = a.ndim
        return pl.BlockSpec(a.shape, lambda b, g, pt: (0,) * nd)

    consts = [cw["w1"], cw["w1f"], cw["pe"], cw["w2p"], cw["w2n"]]
    pad_shape = jax.ShapeDtypeStruct((bs, N_KV, nh, LANES), BF16)
    nat_shape = jax.ShapeDtypeStruct((bs, nh, N_KV * HEAD_DIM), BF16)
    pad_spec = pl.BlockSpec((1, N_KV, nh, LANES), lambda b, g, pt: (b, 0, 0, 0))
    nat_spec = pl.BlockSpec((1, nh, N_KV * HEAD_DIM), lambda b, g, pt: (b, 0, 0))
    grid_spec = pltpu.PrefetchScalarGridSpec(
        num_scalar_prefetch=1,
        grid=(bs, ng),
        in_specs=[page_spec(pg) for pg in range(PAGES_PER_STEP)] + [full(a) for a in consts],
        out_specs=[pad_spec, pad_spec, nat_spec, nat_spec],
        scratch_shapes=[pltpu.VMEM((STRIDE, nh, wcols), F32), pltpu.VMEM((nh + 8, hid), F32)],
    )
    return pl.pallas_call(
        _compress_kernel,
        grid_spec=grid_spec,
        out_shape=[pad_shape, pad_shape, nat_shape, nat_shape],
        compiler_params=_cparams(("arbitrary", "arbitrary")),
        name="compress",
    )(page_table, *([pages] * PAGES_PER_STEP), *consts)


def _qin_prompt_kernel(x_ref, m_ref, g_ref, w_ref, q_ref, gate_ref):
    tr, d = x_ref.shape[1], x_ref.shape[2]
    mod = _mod_getter(m_ref, False)
    h = _rms_mod(x_ref[...], g_ref[...][None], mod(0), mod(1)).reshape(tr, d).astype(BF16)
    a = _dot(h, w_ref[...])
    for hd in range(N_HEADS):
        q_ref[0, hd] = a[:, hd * LANES:(hd + 1) * LANES].astype(BF16)
    gate_ref[0] = a[:, N_HEADS * LANES:]


def _qin_sample_kernel(x_ref, m_ref, g_ref, w_ref, a_ref):
    tn, bk, d = x_ref.shape
    mod = _mod_getter(m_ref, True)
    h = _rms_mod(x_ref[...], g_ref[...][None], mod(0), mod(1)).reshape(tn * bk, d).astype(BF16)
    a_ref[...] = _dot(h, w_ref[...]).reshape(tn, bk, a_ref.shape[-1])


def _qin(x, mods, mods_rows, l, time_major, g1, w):
    tl = _Tiling(x.shape, time_major, mods_rows)
    d = tl.d
    n = w.shape[1]
    common = dict(grid=tl.grid, compiler_params=_cparams(tl.sem))
    in_specs = [tl.x_spec(), tl.mod_spec(6, l), _const_spec((1, d)), _const_spec(w.shape)]
    if time_major:
        return pl.pallas_call(
            _qin_sample_kernel, in_specs=in_specs,
            out_specs=pl.BlockSpec(tl.tile[:2] + (n,), tl.x_map),
            out_shape=jax.ShapeDtypeStruct(x.shape[:2] + (n,), F32),
            name="qin_sample", **common)(x, mods, g1.reshape(1, d), w)
    b, t, _ = x.shape
    tr = tl.rows
    return pl.pallas_call(
        _qin_prompt_kernel, in_specs=in_specs,
        out_specs=[pl.BlockSpec((1, N_HEADS, tr, LANES), lambda bi, i: (bi, 0, i, 0)),
                   pl.BlockSpec((1, tr, LANES), tl.x_map)],
        out_shape=[jax.ShapeDtypeStruct((b, N_HEADS, t, LANES), BF16), jax.ShapeDtypeStruct((b, t, LANES), F32)],
        name="qin_prompt", **common)(x, mods, g1.reshape(1, d), w)


def _block_rank(score_ref, ns):
    score = score_ref[...]
    blk = lax.broadcasted_iota(jnp.int32, score.shape, 0)
    rank = jnp.zeros(score.shape, F32)
    for i in range(ns):
        si = score_ref[i:i + 1, :]
        rank = rank + jnp.where(blk > i, jnp.where(si >= score, 1.0, 0.0), jnp.where(si > score, 1.0, 0.0))
    return rank


def _add_alibi(s, slopes_ref, head0, pos_f, qb):
    return jnp.concatenate([s[g * qb:(g + 1) * qb] + slopes_ref[head0 + g] * pos_f for g in range(GROUP)], axis=0)


def _attn_prompt_kernel(slopes_ref, q_ref, gate_ref, ks_ref, kw_ref, vs_ref, vw_ref, kc_ref, vc_ref, msel_ref,
                        ega_ref, egb_ref, o_ref,
                        s_ref, sc_ref, mt_ref, qa_ref, oc_ref, as_ref, mx_ref, ga_ref, gb_ref):
    qb = q_ref.shape[2]
    rows = GROUP * qb
    t = ks_ref.shape[2]
    nc = kc_ref.shape[2]
    ns = msel_ref.shape[0]
    i = pl.program_id(1)
    s0 = i * qb
    qpos4 = s0 + lax.broadcasted_iota(jnp.int32, (rows, 1), 0) % qb
    low4 = lax.broadcasted_iota(jnp.int32, (rows, LANES), 1) < HEAD_DIM

    gs = jax.nn.sigmoid(gate_ref[0])
    g_hi = gs.astype(BF16)
    g_lo = (gs - g_hi.astype(F32)).astype(BF16)
    g2 = jnp.concatenate([g_hi, g_lo], axis=1)
    ga = _dot(g2, ega_ref[...])
    gb = _dot(g2, egb_ref[...])
    for h in range(N_HEADS):
        ga_ref[h] = ga[:, h * LANES:(h + 1) * LANES]
        gb_ref[h] = gb[:, h * LANES:(h + 1) * LANES]

    cidx = lax.broadcasted_iota(jnp.int32, (1, nc), 1)
    cend = cidx * STRIDE + (L_CMP - 1)
    cmask = (cend <= qpos4) & (cidx < nc - 1)
    cend_f = cend.astype(F32)
    row_ok = qpos4 >= L_CMP - 1
    blk = lax.broadcasted_iota(jnp.int32, (ns, qb), 0)
    qpos_t = s0 + lax.broadcasted_iota(jnp.int32, (ns, qb), 1)
    cur = qpos_t // L_SEL
    valid_t = blk * L_SEL <= qpos_t
    forced = (blk == 0) | (blk == cur) | (blk == cur - 1)

    for kvh in range(N_KV):
        q4 = q_ref[0, kvh * GROUP:(kvh + 1) * GROUP].reshape(rows, LANES)
        sc = _add_alibi(_dot_nt(q4, kc_ref[0, kvh]), slopes_ref, kvh * GROUP, cend_f, qb)
        sc = jnp.where(cmask, sc, NEG)
        e = jnp.exp(sc - jnp.max(sc, axis=1, keepdims=True))
        r = jnp.where(row_ok, 1.0 / jnp.sum(e, axis=1, keepdims=True), 0.0)
        pb = (e * r).astype(BF16)
        oc_ref[kvh] = _dot(pb, vc_ref[0, kvh])
        imp4 = _dot_nt(msel_ref[...], pb)
        imp_t = imp4[:, 0:qb]
        for g in range(1, GROUP):
            imp_t = imp_t + imp4[:, g * qb:(g + 1) * qb]
        sc_ref[kvh] = jnp.where(valid_t, imp_t + jnp.where(forced, FORCE, 0.0), -jnp.inf)
        keep = (_block_rank(sc_ref.at[kvh], ns) < TOP_N) & valid_t
        mt_ref[kvh] = jnp.zeros(mt_ref.shape[1:], F32)
        mt_ref[kvh, HEAD_DIM:HEAD_DIM + ns, :] = jnp.where(keep, 0.0, NEG)
        mask_b = mt_ref[kvh].T.astype(BF16)
        qa_ref[kvh] = jnp.where(low4, q4, jnp.concatenate([mask_b] * GROUP, axis=0))

    n_tiles = (s0 + qb + SEL_TILE - 1) // SEL_TILE
    lane_blocks = SEL_TILE // LANES
    for pair in range(N_KV // 2):
        kvs = (2 * pair, 2 * pair + 1)

        for kvh in kvs:
            mx_ref[kvh] = jnp.full((rows, LANES), NEG, F32)
            as_ref[kvh] = jnp.zeros((rows, LANES), F32)

        def scores(kt, carry, masked):
            k0 = pl.multiple_of(kt * SEL_TILE, SEL_TILE)
            kpos = k0 + lax.broadcasted_iota(jnp.int32, (1, SEL_TILE), 1)
            kpos_f = kpos.astype(F32)
            for j, kvh in enumerate(kvs):
                s = _dot_nt(qa_ref[kvh], ks_ref[0, kvh, pl.ds(k0, SEL_TILE), :])
                s = _add_alibi(s, slopes_ref, kvh * GROUP, kpos_f, qb)
                if masked:
                    s = jnp.where(kpos <= qpos4, s, NEG)
                s_ref[j, kt] = s
                m = s[:, 0:LANES]
                for c in range(1, lane_blocks):
                    m = jnp.maximum(m, s[:, c * LANES:(c + 1) * LANES])
                mx_ref[kvh] = jnp.maximum(mx_ref[kvh], m)
            return carry

        lax.fori_loop(0, n_tiles - 1, lambda kt, c: scores(kt, c, False), 0)
        scores(n_tiles - 1, 0, True)
        mrow = [jnp.max(mx_ref[kvh], axis=1, keepdims=True) for kvh in kvs]

        def weighted(kt, carry):
            k0 = pl.multiple_of(kt * SEL_TILE, SEL_TILE)
            for j, kvh in enumerate(kvs):
                p = jnp.exp(s_ref[j, kt] - mrow[j]).astype(BF16)
                as_ref[kvh] += _dot(p, vs_ref[0, kvh, pl.ds(k0, SEL_TILE), :])
            return carry

        lax.fori_loop(0, n_tiles, weighted, 0)

    wlen = WINDOW + qb
    w0 = pl.multiple_of(jnp.maximum(s0 - WINDOW, 0), qb)
    wpos = w0 + lax.broadcasted_iota(jnp.int32, (1, wlen), 1)
    wpos_f = wpos.astype(F32)
    dist = qpos4 - wpos
    wmask = (dist >= 0) & (dist < WINDOW)
    for kvh in range(N_KV):
        q4 = q_ref[0, kvh * GROUP:(kvh + 1) * GROUP].reshape(rows, LANES)
        s = _add_alibi(_dot_nt(q4, kw_ref[0, kvh, pl.ds(w0, wlen), :]), slopes_ref, kvh * GROUP, wpos_f, qb)
        s = jnp.where(wmask, s, NEG)
        p = jnp.exp(s - jnp.max(s, axis=1, keepdims=True)).astype(BF16)
        acc_w = _dot(p, vw_ref[0, kvh, pl.ds(w0, wlen), :])
        acc_s = as_ref[kvh]
        o_s = acc_s * (1.0 / pltpu.roll(acc_s, HEAD_DIM, 1))
        o_w = acc_w * (1.0 / pltpu.roll(acc_w, HEAD_DIM, 1))
        ga4 = ga_ref[kvh * GROUP:(kvh + 1) * GROUP].reshape(rows, LANES)
        gb4 = gb_ref[kvh * GROUP:(kvh + 1) * GROUP].reshape(rows, LANES)
        out = ga4 * jnp.where(low4, oc_ref[kvh], o_w) + gb4 * jnp.where(low4, o_s, 0.0)
        o_ref[0, kvh * GROUP:(kvh + 1) * GROUP] = out.reshape(GROUP, qb, LANES).astype(BF16)


def _gate_expand():
    src = jnp.arange(2 * LANES) % LANES
    br = (src // N_HEADS)[:, None]
    hd = (src % N_HEADS)[:, None]
    col = jnp.arange(N_HEADS * LANES)
    same = hd == (col // LANES)[None, :]
    low = (col % LANES < HEAD_DIM)[None, :]
    ea = same & (((br == 0) & low) | ((br == 2) & ~low))
    eb = same & (br == 1) & low
    return ea.astype(BF16), eb.astype(BF16)


def _attn_prompt(slopes, q, gates, ks, kw, vs, vw, kcp, vcp, msel_t):
    b, _, t, _ = q.shape
    nc = kcp.shape[2]
    ns = msel_t.shape[0]
    assert t % SEL_TILE == 0 and t >= WINDOW + Q_BLK and ns <= HEAD_DIM and ns % 8 == 0
    ea, eb = _gate_expand()
    rows = GROUP * Q_BLK
    qspec = pl.BlockSpec((1, N_HEADS, Q_BLK, LANES), lambda bi, i: (bi, 0, i, 0))
    kvspec = pl.BlockSpec((1, N_KV, t, LANES), lambda bi, i: (bi, 0, 0, 0), pipeline_mode=pl.Buffered(1))
    cspec = pl.BlockSpec((1, N_KV, nc, LANES), lambda bi, i: (bi, 0, 0, 0))
    return pl.pallas_call(
        _attn_prompt_kernel,
        grid=(b, t // Q_BLK),
        in_specs=[pl.BlockSpec(memory_space=pltpu.SMEM), qspec,
                  pl.BlockSpec((1, Q_BLK, LANES), lambda bi, i: (bi, i, 0)),
                  kvspec, kvspec, kvspec, kvspec, cspec, cspec,
                  _const_spec(msel_t.shape), _const_spec(ea.shape), _const_spec(eb.shape)],
        out_specs=qspec,
        out_shape=jax.ShapeDtypeStruct(q.shape, BF16),
        scratch_shapes=[pltpu.VMEM((2, t // SEL_TILE, rows, SEL_TILE), F32),
                        pltpu.VMEM((N_KV, ns, Q_BLK), F32), pltpu.VMEM((N_KV, LANES, Q_BLK), F32),
                        pltpu.VMEM((N_KV, rows, LANES), BF16), pltpu.VMEM((N_KV, rows, LANES), F32),
                        pltpu.VMEM((N_KV, rows, LANES), F32), pltpu.VMEM((N_KV, rows, LANES), F32),
                        pltpu.VMEM((N_HEADS, Q_BLK, LANES), F32), pltpu.VMEM((N_HEADS, Q_BLK, LANES), F32)],
        compiler_params=_cparams(("arbitrary", "arbitrary")),
        name="attn_prompt",
    )(slopes, q, gates, ks, kw, vs, vw, kcp, vcp, msel_t, ea, eb)


def _attn_sample_kernel(pt_ref, *refs, past):
    del pt_ref
    npg = past // PAGE
    pages = refs[:npg]
    (q_ref, gate_ref, kc_ref, vc_ref, sw_ref, kvn_ref, slope_ref, msel_ref, o_ref,
     s_ref, sc_ref, st_ref) = refs[npg:]
    tn = q_ref.shape[1]
    hw = N_KV * HEAD_DIM
    rows = N_KV * GROUP * tn
    nc = kc_ref.shape[1]
    ns_pad = sc_ref.shape[0]
    nblk = npg + 1
    n_s = (past + tn + L_SEL - 1) // L_SEL
    f_sel = 2 * hw
    slope = slope_ref[...]

    rr = lax.broadcasted_iota(jnp.int32, (rows, 1), 0)
    qpos = past + rr % tn
    col = lax.broadcasted_iota(jnp.int32, (tn, hw), 1) // HEAD_DIM
    colr = lax.broadcasted_iota(jnp.int32, (rows, hw), 1) // HEAD_DIM
    rowk = lax.broadcasted_iota(jnp.int32, (rows, hw), 0) // (GROUP * tn)
    diag = colr == rowk

    qs = q_ref[0]
    pieces = []
    for k in range(N_KV):
        for g in range(GROUP):
            pieces.append(jnp.where(col == k, qs[:, g * hw:(g + 1) * hw], 0.0))
    qbd = jnp.concatenate(pieces, axis=0).astype(BF16)

    cidx = lax.broadcasted_iota(jnp.int32, (1, nc), 1)
    cend = cidx * STRIDE + (L_CMP - 1)
    cvalid = (cend <= qpos) & (cidx < nc - 1)
    sc = jnp.where(cvalid, _dot_nt(qbd, kc_ref[0]) + slope * cend.astype(F32), NEG)
    e = jnp.exp(sc - jnp.max(sc, axis=1, keepdims=True))
    r = jnp.where(qpos >= L_CMP - 1, 1.0 / jnp.sum(e, axis=1, keepdims=True), 0.0)
    pc = (e * r).astype(BF16)
    o_c = _dot(pc, vc_ref[0])

    imp = _dot_nt(msel_ref[...], pc)
    tot = imp
    for g in range(1, GROUP):
        tot = tot + pltpu.roll(imp, rows - g * tn, 1)
    blk = lax.broadcasted_iota(jnp.int32, (ns_pad, rows), 0)
    lane_r = lax.broadcasted_iota(jnp.int32, (ns_pad, rows), 1)
    qpos_l = past + lane_r % tn
    cur = qpos_l // L_SEL
    valid_l = (blk * L_SEL <= qpos_l) & (blk < n_s)
    forced = (blk == 0) | (blk == cur) | (blk == cur - 1)
    sc_ref[...] = jnp.where(valid_l, tot + jnp.where(forced, FORCE, 0.0), -jnp.inf)
    keep = (_block_rank(sc_ref, n_s) < TOP_N) & valid_l
    lead = (lane_r // tn) % GROUP == 0
    keep_f = jnp.where(keep & lead, 1.0, 0.0)
    spread = keep_f
    for g in range(1, GROUP):
        spread = spread + pltpu.roll(keep_f, g * tn, 1)
    st_ref[...] = jnp.zeros(st_ref.shape, F32)
    st_ref[0:ns_pad, :] = spread
    keep_rows = st_ref[...].T.astype(BF16)

    zpad = jnp.zeros((PAGE - tn, hw), F32)
    bsel = lax.broadcasted_iota(jnp.int32, (LANES, PAGE), 0)
    ksub = lax.broadcasted_iota(jnp.int32, (LANES, PAGE), 1) // L_SEL
    kloc = lax.broadcasted_iota(jnp.int32, (1, PAGE), 1)
    mrun = jnp.full((rows, 1), NEG, F32)
    for j in range(nblk):
        if j < npg:
            qk = _dot(qbd, pages[j][0, 0:hw, :].astype(BF16))
        else:
            kb = jnp.concatenate([kvn_ref[0, :, f_sel:f_sel + hw], zpad], axis=0).astype(BF16)
            qk = _dot_nt(qbd, kb)
        kpos = j * PAGE + kloc
        expand = jnp.where(bsel == (PAGE // L_SEL) * j + ksub, 1.0, 0.0).astype(BF16)
        kept = _dot(keep_rows, expand)
        s = jnp.where((kept > 0.5) & (kpos <= qpos), qk + slope * kpos.astype(F32), NEG)
        s_ref[:, j * PAGE:(j + 1) * PAGE] = s
        mrun = jnp.maximum(mrun, jnp.max(s, axis=1, keepdims=True))
    lsum = jnp.zeros((rows, 1), F32)
    o_s = jnp.zeros((rows, hw), F32)
    for j in range(nblk):
        p = jnp.exp(s_ref[:, j * PAGE:(j + 1) * PAGE] - mrun)
        lsum = lsum + jnp.sum(p, axis=1, keepdims=True)
        if j < npg:
            o_s = o_s + _dot_nt(p.astype(BF16), pages[j][0, hw:2 * hw, :].astype(BF16))
        else:
            vb = jnp.concatenate([kvn_ref[0, :, f_sel + hw:f_sel + 2 * hw], zpad], axis=0).astype(BF16)
            o_s = o_s + _dot(p.astype(BF16), vb)
    o_s = o_s * (1.0 / lsum)

    nbuf = sw_ref.shape[2]
    f_win = 4 * hw
    kwn = jnp.concatenate([kvn_ref[0, :, f_win:f_win + hw], zpad], axis=0).astype(BF16)
    vwn = jnp.concatenate([kvn_ref[0, :, f_win + hw:f_win + 2 * hw], zpad], axis=0).astype(BF16)
    pos_a = past - nbuf + lax.broadcasted_iota(jnp.int32, (1, nbuf), 1)
    pos_b = past + kloc
    da = qpos - pos_a
    db = qpos - pos_b
    va = (da >= 0) & (da < WINDOW) & (pos_a >= 0)
    vb_ok = (db >= 0) & (db < WINDOW)
    s_a = jnp.where(va, _dot(qbd, sw_ref[0, 0:hw, :].astype(BF16)) + slope * pos_a.astype(F32), NEG)
    s_b = jnp.where(vb_ok, _dot_nt(qbd, kwn) + slope * pos_b.astype(F32), NEG)
    m = jnp.maximum(jnp.max(s_a, axis=1, keepdims=True), jnp.max(s_b, axis=1, keepdims=True))
    p_a = jnp.exp(s_a - m)
    p_b = jnp.exp(s_b - m)
    l_w = jnp.sum(p_a, axis=1, keepdims=True) + jnp.sum(p_b, axis=1, keepdims=True)
    o_w = (_dot_nt(p_a.astype(BF16), sw_ref[0, hw:2 * hw, :].astype(BF16)) + _dot(p_b.astype(BF16), vwn)) * (1.0 / l_w)

    gt = jax.nn.sigmoid(gate_ref[0])
    tot_o = jnp.where(diag, gt[0] * o_c + gt[1] * o_s + gt[2] * o_w, 0.0)
    for g in range(GROUP):
        acc = jnp.zeros((tn, hw), F32)
        for k in range(N_KV):
            r0 = (k * GROUP + g) * tn
            acc = acc + tot_o[r0:r0 + tn, :]
        o_ref[0, :, g * hw:(g + 1) * hw] = acc


def _attn_sample(page_table, cache_t, q, gates, kcn, vcn, state_win_t, kv_new, slope_rows, msel_t, past):
    bd, tn, qw = q.shape
    npg = past // PAGE
    hw = N_KV * HEAD_DIM
    rows = N_KV * GROUP * tn
    assert rows == LANES and msel_t.shape[1] == kcn.shape[1]
    ns_pad = msel_t.shape[0]

    def page_spec(pg):
        return pl.BlockSpec((1, 2 * hw, PAGE), lambda b, pt: (pt[b, pg], 1, 0))

    def per_seq(a):
        nd = a.ndim
        return pl.BlockSpec((1,) + a.shape[1:], lambda b, pt: (b,) + (0,) * (nd - 1))

    def full(a):
        nd = a.ndim
        return pl.BlockSpec(a.shape, lambda b, pt: (0,) * nd)

    grid_spec = pltpu.PrefetchScalarGridSpec(
        num_scalar_prefetch=1,
        grid=(bd,),
        in_specs=[page_spec(pg) for pg in range(npg)]
        + [per_seq(q), per_seq(gates), per_seq(kcn), per_seq(vcn), per_seq(state_win_t), per_seq(kv_new),
           full(slope_rows), full(msel_t)],
        out_specs=pl.BlockSpec((1, tn, N_HEADS * HEAD_DIM), lambda b, pt: (b, 0, 0)),
        scratch_shapes=[pltpu.VMEM((rows, (npg + 1) * PAGE), F32), pltpu.VMEM((ns_pad, rows), F32),
                        pltpu.VMEM((LANES, rows), F32)],
    )
    return pl.pallas_call(
        functools.partial(_attn_sample_kernel, past=past),
        grid_spec=grid_spec,
        out_shape=jax.ShapeDtypeStruct((bd, tn, N_HEADS * HEAD_DIM), F32),
        compiler_params=_cparams(("arbitrary",)),
        name="attn_sample",
    )(page_table, *([cache_t] * npg), q, gates, kcn, vcn, state_win_t, kv_new, slope_rows, msel_t)


def _msel_t(n_c_rows, n_s_rows):
    i0 = jnp.arange(n_c_rows)[None, :] * STRIDE
    j0 = jnp.arange(n_s_rows)[:, None] * L_SEL
    return ((i0 < j0 + L_SEL) & (i0 + L_CMP > j0)).astype(BF16)


def _prep_ffn(w_up, w_down):
    d, two_ff = w_up.shape
    ff = two_ff // 2
    n = ff // FF_CHUNK
    assert n * FF_CHUNK == ff
    a = w_up[:, :ff].reshape(d, n, FF_CHUNK)
    b = w_up[:, ff:].reshape(d, n, FF_CHUNK)
    wup_c = jnp.concatenate([a, b], axis=2).transpose(1, 0, 2).astype(BF16)
    wdn_c = w_down.reshape(n, FF_CHUNK, w_down.shape[1]).astype(BF16)
    return wup_c, wdn_c


def _prep_compress(w_cmp1, w_cmp2, pe_cmp):
    hid = w_cmp1.shape[-1]
    wl = jnp.concatenate([w_cmp1[:, :STRIDE], w_cmp1[:, STRIDE:]], axis=-1)
    z = jnp.zeros_like(wl)
    w1 = jnp.concatenate([jnp.concatenate([wl, z], axis=-1), jnp.concatenate([z, wl], axis=-1)], axis=2)
    w1f = w_cmp1.reshape(2, L_CMP * HEAD_DIM, hid)
    pe = jnp.zeros((2, 8, L_CMP * HEAD_DIM), F32).at[:, 0].set(pe_cmp.reshape(2, -1))
    w2p = jnp.concatenate([w_cmp2, jnp.zeros_like(w_cmp2)], axis=-1)
    w2n = jnp.zeros((2, N_KV, hid, N_KV * HEAD_DIM), F32)
    for k in range(N_KV):
        w2n = w2n.at[:, k, :, k * HEAD_DIM:(k + 1) * HEAD_DIM].set(w_cmp2)
    return dict(w1=w1.astype(BF16), w1f=w1f.astype(BF16), pe=pe.astype(BF16), w2p=w2p.astype(BF16),
                w2n=w2n.astype(BF16))


def _prep_kv_perm(w_kv):
    d = w_kv.shape[0]
    w6 = w_kv.reshape(d, 6, N_KV, HEAD_DIM)
    z = jnp.zeros((d, N_KV, HEAD_DIM), w_kv.dtype)
    ks = jnp.concatenate([w6[:, 2], z], axis=-1)
    kw = jnp.concatenate([w6[:, 4], z], axis=-1)
    vs = jnp.concatenate([w6[:, 3], z], axis=-1)
    vw = jnp.concatenate([z, w6[:, 5]], axis=-1)
    return jnp.concatenate([ks, kw, vs, vw], axis=1).reshape(d, 4 * N_KV * LANES).astype(BF16)


def _prep_attn_prompt(w_in, w_o):
    d = w_in.shape[0]
    qw = N_HEADS * HEAD_DIM
    scale = HEAD_DIM ** -0.5
    wq = (w_in[:, :qw] * scale).reshape(d, N_HEADS, HEAD_DIM)
    wq = jnp.concatenate([wq, jnp.zeros_like(wq)], axis=-1).reshape(d, N_HEADS * LANES)
    wg = w_in[:, qw:].reshape(d, N_HEADS, 3).transpose(0, 2, 1).reshape(d, 3 * N_HEADS)
    wg = jnp.concatenate([wg, jnp.zeros((d, LANES - 3 * N_HEADS), w_in.dtype)], axis=1)
    w_in_p = jnp.concatenate([wq, wg], axis=1).astype(BF16)
    wo3 = w_o.reshape(N_HEADS, HEAD_DIM, w_o.shape[1])
    wo_p = jnp.concatenate([wo3, wo3], axis=1).reshape(N_HEADS * LANES, w_o.shape[1]).astype(BF16)
    return w_in_p, wo_p


def _prep_attn_sample(w_in, w_o):
    d = w_in.shape[0]
    qw = N_HEADS * HEAD_DIM
    scale = HEAD_DIM ** -0.5
    wq = (w_in[:, :qw] * scale).reshape(d, N_KV, GROUP, HEAD_DIM).transpose(0, 2, 1, 3).reshape(d, qw)
    w_in_s = jnp.concatenate([wq, w_in[:, qw:]], axis=1).astype(BF16)
    wo_s = w_o.reshape(N_KV, GROUP, HEAD_DIM, w_o.shape[1]).transpose(1, 0, 2, 3).reshape(qw, w_o.shape[1])
    return w_in_s, wo_s.astype(BF16)


def kernel(x_prompt, x_sample, cache_kv, page_table, state_win, state_conv, c_prompt, c_sample, w_ada, b_ada, norm_g, conv_w_pw1, conv_b_pw1, conv_w_dw, conv_b_dw, conv_ln_g, conv_ln_b, conv_w_pw2, ffn_w_up, ffn_w_down, kv_norm_g, w_kv, w_cmp1, w_cmp2, pe_cmp, nsa_w_in, nsa_w_o, final_norm_g):
    b, t, d = x_prompt.shape
    bd, tn, _ = x_sample.shape
    depth = norm_g.shape[0]
    n_a = conv_w_pw1.shape[0]
    n_mod = b_ada.shape[0] // d
    past = page_table.shape[1] * PAGE
    hw = N_KV * HEAD_DIM
    assert bd % 8 == 0 and b <= 8 and n_mod == 6 * depth + 4

    c_all = jnp.concatenate([c_sample, c_prompt, jnp.zeros((8 - b % 8, d), F32)], axis=0)
    mods = _mods(c_all, w_ada, b_ada, n_mod)

    ffn_w = [_prep_ffn(ffn_w_up[l], ffn_w_down[l]) for l in range(depth)]
    w_pw1 = conv_w_pw1.astype(BF16)
    w_pw2 = conv_w_pw2.astype(BF16)
    w_kv_t = w_kv.T.astype(BF16)
    w_kv_perm = _prep_kv_perm(w_kv)
    cw = _prep_compress(w_cmp1, w_cmp2, pe_cmp)
    slopes = jnp.exp2(-8.0 * jnp.arange(1, N_HEADS + 1, dtype=F32) / N_HEADS)
    kblk_kv = (6 * depth) // 2
    kblk_final = (6 * depth + 2) // 2

    def trunk(x, time_major, conv_prev_tm, attn_fn_builder):
        conv_states = []
        attn_fn = None
        extra = None
        for l in range(depth):
            if l == n_a:
                attn_fn, extra = attn_fn_builder(x)
            if l < n_a:
                x, st = _conv_layer(x, None if conv_prev_tm is None else conv_prev_tm[l], mods, bd, l, time_major,
                                    norm_g[l, 0], w_pw1[l], conv_b_pw1[l], conv_w_dw[l], conv_b_dw[l],
                                    conv_ln_g[l], conv_ln_b[l], w_pw2[l])
                conv_states.append(st)
                o = wo = None
            else:
                o, wo = attn_fn(x, l)
            last = l == depth - 1
            x = _ffn_layer(x, mods, bd, l, time_major, norm_g[l, 1], ffn_w[l][0], ffn_w[l][1], o=o, wo=wo,
                           final_g=final_norm_g if last else None, n_mod_final=kblk_final)
        return x, extra, conv_states

    def from_feature_major(a, lead):
        nl = len(lead)
        a = a.reshape(lead + (-1, N_KV, HEAD_DIM, a.shape[-1]))
        return a.transpose(tuple(range(nl)) + (nl + 3, nl, nl + 1, nl + 2))

    attn_w_p = [_prep_attn_prompt(nsa_w_in[i], nsa_w_o[i]) for i in range(depth - n_a)]

    def prompt_attn_builder(x):
        kvt_pages, win_t, ks, kw, vs, vw = _kv_proj(x, mods, bd, kblk_kv, False, kv_norm_g, w_kv_t, w_kv_perm)
        ident = jnp.arange(b * t // PAGE, dtype=jnp.int32).reshape(b, t // PAGE)
        kcp, vcp, _, _ = _compress(kvt_pages, ident, cw)
        msel_t = _msel_t(t // STRIDE, t // L_SEL)

        def attn(xl, l):
            w_in_p, wo_p = attn_w_p[l - n_a]
            q, gates = _qin(xl, mods, bd, l, False, norm_g[l, 0], w_in_p)
            return _attn_prompt(slopes, q, gates, ks, kw, vs, vw, kcp, vcp, msel_t), wo_p

        extra = (from_feature_major(kvt_pages, (b * t // PAGE,)), from_feature_major(win_t, (b,)))
        return attn, extra

    y_prompt, (kv_prompt, win_prompt), conv_p = trunk(x_prompt, False, None, prompt_attn_builder)
    conv_prompt = jnp.stack([st[:, 32 - (CONV_W - 1):] for st in conv_p])

    attn_w_s = [_prep_attn_sample(nsa_w_in[i], nsa_w_o[i]) for i in range(depth - n_a)]
    cache_t = cache_kv.transpose(0, 2, 3, 4, 1).reshape(cache_kv.shape[0], 4 * hw, PAGE)
    nbuf = state_win.shape[1]
    state_win_t = state_win.transpose(0, 2, 3, 4, 1).reshape(bd, 2 * hw, nbuf)
    slope_rows = jnp.repeat(slopes, tn).reshape(N_HEADS * tn, 1)

    def sample_attn_builder(x):
        kvt = _kv_proj(x, mods, bd, kblk_kv, True, kv_norm_g, w_kv_t, None)
        kvt3 = kvt.reshape(6 * hw, tn, bd)
        kv_new = kvt3.transpose(2, 1, 0)
        _, _, kcn, vcn = _compress(cache_t, page_table, cw)
        n_s = -(-(past + tn) // L_SEL)
        msel_t = _msel_t(kcn.shape[1], -(-n_s // 8) * 8)

        def attn(xl, l):
            w_in_s, wo_s = attn_w_s[l - n_a]
            a = _qin(xl, mods, bd, l, True, norm_g[l, 0], w_in_s).transpose(1, 0, 2)
            q = a[..., :N_HEADS * HEAD_DIM]
            gates = a[..., N_HEADS * HEAD_DIM:].reshape(bd, tn, N_HEADS, 3).transpose(0, 3, 2, 1)
            gates = gates.reshape(bd, 3, N_HEADS * tn, 1)
            o = _attn_sample(page_table, cache_t, q, gates, kcn, vcn, state_win_t, kv_new, slope_rows, msel_t,
                             past)
            return o.transpose(1, 0, 2), wo_s

        n_keep = min(WINDOW, past + tn)
        win_all_t = jnp.concatenate([state_win_t, kvt3[4 * hw:].transpose(2, 0, 1)], axis=2)[:, :, -n_keep:]
        kv_sample = kvt3[:4 * hw].reshape(4, N_KV, HEAD_DIM, tn, bd).transpose(4, 3, 0, 1, 2)
        return attn, (kv_sample, from_feature_major(win_all_t, (bd,)))

    conv_prev_tm = state_conv.transpose(0, 2, 1, 3)
    y_s_tm, (kv_sample, win_sample), conv_s = trunk(x_sample.transpose(1, 0, 2), True, conv_prev_tm,
                                                    sample_attn_builder)
    y_sample = y_s_tm.transpose(1, 0, 2)
    conv_sample = jnp.stack(conv_s).transpose(0, 2, 1, 3)

    return (y_prompt, y_sample, kv_prompt, kv_sample, win_prompt, win_sample, conv_prompt, conv_sample)
```

```python
import functools

import jax
import jax.numpy as jnp
from jax import lax
from jax.experimental import pallas as pl
from jax.experimental.pallas import tpu as pltpu

F32 = jnp.float32
BF16 = jnp.bfloat16

N_HEADS = 16
N_KV = 4
GROUP = N_HEADS // N_KV
HEAD_DIM = 64
L_CMP = 32
STRIDE = 16
L_SEL = 64
TOP_N = 16
WINDOW = 512
Q_BLK = 128
PAGE = 128
CONV_W = 31
EPS = 1e-6
FORCE = 1e3
NEG = -1e30

LANES = 128
SUBLANES = 8
FF_CHUNK = 256
SEL_TILE = 256
PAGES_PER_STEP = 16
VMEM_LIMIT = 56 * 1024 * 1024


def _cparams(sem):
    return pltpu.CompilerParams(dimension_semantics=sem, vmem_limit_bytes=VMEM_LIMIT)


def _const_spec(shape):
    nd = len(shape)
    return pl.BlockSpec(shape, lambda *_: (0,) * nd, pipeline_mode=pl.Buffered(1))


def _dot(a, b):
    return jnp.dot(a, b, preferred_element_type=F32)


def _dot_nt(a, b):
    return lax.dot_general(a, b, (((1,), (1,)), ((), ())), preferred_element_type=F32)


def _silu(x):
    return x * jax.nn.sigmoid(x)


def _rms_mod(x, g, shift, scale):
    ms = jnp.mean(x * x, axis=-1, keepdims=True)
    y = x * lax.rsqrt(ms + EPS) * g
    return y * (1.0 + scale) + shift


def _mod_getter(m_ref, time_major):
    if time_major:
        return lambda k: m_ref[k][None]
    b = pl.program_id(0)
    return lambda k: m_ref[k, pl.ds(b, 1), :][None]


def _mods_kernel(c_ref, w_ref, b_ref, o_ref):
    c = c_ref[...]
    s = _silu(c).astype(BF16)
    o_ref[0] = _dot(s, w_ref[...].astype(BF16)) + b_ref[0]


def _mods(c_all, w_ada, b_ada, n_mod):
    r, d = c_all.shape
    return pl.pallas_call(
        _mods_kernel,
        grid=(n_mod,),
        in_specs=[pl.BlockSpec((r, d), lambda k: (0, 0)),
                  pl.BlockSpec((d, d), lambda k: (0, k)),
                  pl.BlockSpec((1, 1, d), lambda k: (k, 0, 0))],
        out_specs=pl.BlockSpec((1, r, d), lambda k: (k, 0, 0)),
        out_shape=jax.ShapeDtypeStruct((n_mod, r, d), F32),
        compiler_params=_cparams(("arbitrary",)),
        name="mods",
    )(c_all, w_ada, b_ada.reshape(n_mod, 1, d))


class _Tiling:
    def __init__(self, x_shape, time_major, mods_rows, bk=None):
        self.time_major = time_major
        if time_major:
            tn, bd, d = x_shape
            bk = min(bd, 32) if bk is None else bk
            assert bd % bk == 0 and bk % 8 == 0
            self.grid = (bd // bk,)
            self.tile = (tn, bk, d)
            self.rows = tn * bk
            self.x_map = lambda j: (0, j, 0)
            self.mod_block = lambda k: (k, bk, d)
            self.mod_map = lambda kblk: (lambda j: (kblk, j, 0))
            self.sem = ("arbitrary",)
        else:
            b, t, d = x_shape
            tr = min(t, 512)
            assert t % tr == 0
            self.grid = (b, t // tr)
            self.tile = (1, tr, d)
            self.rows = tr
            self.x_map = lambda bi, i: (bi, i, 0)
            self.mod_block = lambda k: (k, 8, d)
            self.mod_map = lambda kblk: (lambda bi, i: (kblk, mods_rows // 8, 0))
            self.sem = ("arbitrary", "arbitrary")
        self.d = d

    def x_spec(self):
        return pl.BlockSpec(self.tile, self.x_map)

    def mod_spec(self, k, kblk):
        return pl.BlockSpec(self.mod_block(k), self.mod_map(kblk))


def _conv_core(y, ln_g, ln_b):
    mu = jnp.mean(y, axis=-1, keepdims=True)
    yc = y - mu
    var = jnp.mean(yc * yc, axis=-1, keepdims=True)
    return _silu(yc * lax.rsqrt(var + EPS) * ln_g + ln_b)


def _glu_rows(h2, w1_ref, b1_ref, d):
    a1 = _dot(h2, w1_ref[:, :d]) + b1_ref[:, :d]
    a2 = _dot(h2, w1_ref[:, d:]) + b1_ref[:, d:]
    return a1 * jax.nn.sigmoid(a2)


def _conv_prompt_kernel(x_ref, m_ref, g_ref, w1_ref, b1_ref, wdw_ref, bdw_ref, lng_ref, lnb_ref, w2_ref,
                        xo_ref, st_ref, s_ref, p_ref):
    tr, d = x_ref.shape[1], x_ref.shape[2]
    halo = 32
    off = halo - (CONV_W - 1)
    mod = _mod_getter(m_ref, False)

    @pl.when(pl.program_id(1) == 0)
    def _():
        s_ref[0:halo, :] = jnp.zeros((halo, d), F32)
        s_ref[halo + tr:halo + tr + SUBLANES, :] = jnp.zeros((SUBLANES, d), F32)

    x = x_ref[...]
    h = _rms_mod(x, g_ref[...][None], mod(0), mod(1))
    s_ref[halo:halo + tr, :] = _glu_rows(h.reshape(tr, d).astype(BF16), w1_ref, b1_ref, d)

    acc = jnp.broadcast_to(bdw_ref[...], (tr, d))
    for a in range(SUBLANES):
        part = None
        for r in range(a, off + CONV_W, SUBLANES):
            if r >= off:
                term = wdw_ref[r - off:r - off + 1, :] * s_ref[r - a:r - a + tr + SUBLANES, :]
                part = term if part is None else part + term
        if a == 0:
            acc = acc + part[0:tr]
        else:
            p_ref[...] = part
            acc = acc + p_ref[pl.ds(a, tr), :]
    z = _conv_core(acc, lng_ref[...], lnb_ref[...]).astype(BF16)
    y2 = _dot(z, w2_ref[...])
    xo_ref[...] = x + mod(2) * y2[None]
    tail = s_ref[tr:tr + halo, :]
    st_ref[0] = tail
    s_ref[0:halo, :] = tail


def _conv_sample_kernel(x_ref, p_ref, m_ref, g_ref, w1_ref, b1_ref, wdw_ref, bdw_ref, lng_ref, lnb_ref, w2_ref,
                        xo_ref, st_ref, s_ref):
    tn, bk, d = x_ref.shape
    npv = CONV_W - 1
    mod = _mod_getter(m_ref, True)
    x = x_ref[...]
    h = _rms_mod(x, g_ref[...][None], mod(0), mod(1))
    u = _glu_rows(h.reshape(tn * bk, d).astype(BF16), w1_ref, b1_ref, d).reshape(tn, bk, d)
    s_ref[0:npv] = p_ref[...]
    s_ref[npv:npv + tn] = u
    acc = jnp.broadcast_to(bdw_ref[...][None], (tn, bk, d))
    for w in range(CONV_W):
        acc = acc + wdw_ref[w:w + 1, :][None] * s_ref[w:w + tn]
    z = _conv_core(acc, lng_ref[...][None], lnb_ref[...][None]).astype(BF16)
    y2 = _dot(z.reshape(tn * bk, d), w2_ref[...]).reshape(tn, bk, d)
    xo_ref[...] = x + mod(2) * y2
    st_ref[...] = s_ref[tn:tn + npv]


def _conv_layer(x, prev_tm, mods, mods_rows, l, time_major, g1, w1, b1, wdw, bdw, lng, lnb, w2):
    tl = _Tiling(x.shape, time_major, mods_rows)
    d = tl.d
    weights = [g1.reshape(1, d), w1, b1.reshape(1, 2 * d), wdw, bdw.reshape(1, d), lng.reshape(1, d),
               lnb.reshape(1, d), w2]
    w_specs = [_const_spec(w.shape) for w in weights]
    if time_major:
        tn, bd, _ = x.shape
        bk = tl.tile[1]
        npv = CONV_W - 1
        return pl.pallas_call(
            _conv_sample_kernel,
            grid=tl.grid,
            in_specs=[tl.x_spec(), pl.BlockSpec((npv, bk, d), lambda j: (0, j, 0)), tl.mod_spec(6, l)] + w_specs,
            out_specs=[tl.x_spec(), pl.BlockSpec((npv, bk, d), lambda j: (0, j, 0))],
            out_shape=[jax.ShapeDtypeStruct(x.shape, F32), jax.ShapeDtypeStruct((npv, bd, d), F32)],
            scratch_shapes=[pltpu.VMEM((npv + tn, bk, d), F32)],
            compiler_params=_cparams(tl.sem),
            name="conv_sample",
        )(x, prev_tm, mods, *weights)
    b, t, _ = x.shape
    tr = tl.rows
    return pl.pallas_call(
        _conv_prompt_kernel,
        grid=tl.grid,
        in_specs=[tl.x_spec(), tl.mod_spec(6, l)] + w_specs,
        out_specs=[tl.x_spec(), pl.BlockSpec((1, 32, d), lambda bi, i: (bi, 0, 0))],
        out_shape=[jax.ShapeDtypeStruct(x.shape, F32), jax.ShapeDtypeStruct((b, 32, d), F32)],
        scratch_shapes=[pltpu.VMEM((tr + 32 + SUBLANES, d), F32), pltpu.VMEM((tr + SUBLANES, d), F32)],
        compiler_params=_cparams(tl.sem),
        name="conv_prompt",
    )(x, mods, *weights)


def _ffn_kernel(*refs, time_major, has_attn, final, head_major_o):
    it = iter(refs)
    x_ref = next(it)
    m_ref = next(it)
    g2_ref = next(it)
    if has_attn:
        o_ref = next(it)
        wo_ref = next(it)
    wup_ref = next(it)
    wdn_ref = next(it)
    if final:
        fm_ref = next(it)
        fg_ref = next(it)
    out_ref = next(it)
    hb_ref = next(it)
    acc_ref = next(it)

    a, bk, d = x_ref.shape
    rows = a * bk
    mod = _mod_getter(m_ref, time_major)
    x = x_ref[...]
    if has_attn:
        if head_major_o:
            ocat = jnp.concatenate([o_ref[0, h] for h in range(N_HEADS)], axis=1)
        else:
            ocat = o_ref[...].reshape(rows, o_ref.shape[-1]).astype(BF16)
        x = x + mod(2) * _dot(ocat, wo_ref[...]).reshape(a, bk, d)
    h = _rms_mod(x, g2_ref[...][None], mod(3), mod(4))
    hb_ref[...] = h.reshape(rows, d).astype(BF16)
    acc_ref[...] = jnp.zeros((rows, d), F32)
    c = wdn_ref.shape[1]

    def body(j, carry):
        t = _dot(hb_ref[...], wup_ref[j])
        act = (_silu(t[:, :c]) * t[:, c:]).astype(BF16)
        acc_ref[...] += _dot(act, wdn_ref[j])
        return carry

    lax.fori_loop(0, wup_ref.shape[0], body, 0)
    x = x + mod(5) * acc_ref[...].reshape(a, bk, d)
    if final:
        fmod = _mod_getter(fm_ref, time_major)
        x = _rms_mod(x, fg_ref[...][None], fmod(0), fmod(1))
    out_ref[...] = x


def _ffn_layer(x, mods, mods_rows, l, time_major, g2, wup_c, wdn_c, o=None, wo=None, final_g=None, n_mod_final=None):
    tl = _Tiling(x.shape, time_major, mods_rows)
    d = tl.d
    has_attn = o is not None
    final = final_g is not None
    args = [x, mods, g2.reshape(1, d)]
    specs = [tl.x_spec(), tl.mod_spec(6, l), _const_spec((1, d))]
    head_major_o = False
    if has_attn:
        if time_major:
            specs.append(pl.BlockSpec(tl.tile[:2] + (o.shape[-1],), tl.x_map))
        else:
            head_major_o = True
            tr = tl.rows
            specs.append(pl.BlockSpec((1, N_HEADS, tr, LANES), lambda bi, i: (bi, 0, i, 0)))
        args += [o, wo]
        specs.append(_const_spec(wo.shape))
    args += [wup_c, wdn_c]
    specs += [_const_spec(wup_c.shape), _const_spec(wdn_c.shape)]
    if final:
        args += [mods, final_g.reshape(1, d)]
        specs += [tl.mod_spec(2, n_mod_final), _const_spec((1, d))]
    kern = functools.partial(_ffn_kernel, time_major=time_major, has_attn=has_attn, final=final,
                             head_major_o=head_major_o)
    return pl.pallas_call(
        kern,
        grid=tl.grid,
        in_specs=specs,
        out_specs=tl.x_spec(),
        out_shape=jax.ShapeDtypeStruct(x.shape, F32),
        scratch_shapes=[pltpu.VMEM((tl.rows, d), BF16), pltpu.VMEM((tl.rows, d), F32)],
        compiler_params=_cparams(tl.sem),
        name="ffn_sample" if time_major else "ffn_prompt",
    )(*args)


def _kv_prompt_kernel(x_ref, m_ref, g_ref, wt_ref, wp_ref, kvt_ref, wint_ref, ks_ref, kw_ref, vs_ref, vw_ref):
    tr, d = x_ref.shape[1], x_ref.shape[2]
    mod = _mod_getter(m_ref, False)
    hk = _rms_mod(x_ref[...], g_ref[...][None], mod(0), mod(1)).reshape(tr, d).astype(BF16)
    kvt = _dot_nt(wt_ref[...], hk)
    n4 = kvt_ref.shape[1]
    for p in range(tr // PAGE):
        kvt_ref[p] = kvt[:n4, p * PAGE:(p + 1) * PAGE]
    wint_ref[0] = kvt[n4:, :]
    att = _dot(hk, wp_ref[...])
    t0 = pl.program_id(1) * tr
    blk = (t0 + lax.broadcasted_iota(jnp.int32, (tr, LANES), 0)) // L_SEL
    lane = lax.broadcasted_iota(jnp.int32, (tr, LANES), 1)
    onehot = lane - HEAD_DIM == blk
    low = lane < HEAD_DIM
    for k in range(N_KV):
        def part(j):
            return att[:, (j * N_KV + k) * LANES:(j * N_KV + k + 1) * LANES]
        ks_ref[0, k] = jnp.where(onehot, 1.0, part(0)).astype(BF16)
        kw_ref[0, k] = part(1).astype(BF16)
        vs_ref[0, k] = jnp.where(low, part(2), 1.0).astype(BF16)
        vw_ref[0, k] = jnp.where(low, 1.0, part(3)).astype(BF16)


def _kv_sample_kernel(x_ref, m_ref, g_ref, wt_ref, kvt_ref):
    tn, bk, d = x_ref.shape
    mod = _mod_getter(m_ref, True)
    hk = _rms_mod(x_ref[...], g_ref[...][None], mod(0), mod(1)).reshape(tn * bk, d).astype(BF16)
    kvt_ref[...] = _dot_nt(wt_ref[...], hk)


def _kv_proj(x, mods, mods_rows, kblk, time_major, g, w_t, w_perm):
    d = x.shape[-1]
    nkv = w_t.shape[0]
    if time_major:
        tn, bd, _ = x.shape
        tl = _Tiling(x.shape, True, mods_rows, bk=bd)
        return pl.pallas_call(
            _kv_sample_kernel,
            grid=tl.grid,
            in_specs=[tl.x_spec(), tl.mod_spec(2, kblk), _const_spec((1, d)), _const_spec(w_t.shape)],
            out_specs=pl.BlockSpec((nkv, tn * bd), lambda j: (0, 0)),
            out_shape=jax.ShapeDtypeStruct((nkv, tn * bd), F32),
            compiler_params=_cparams(tl.sem),
            name="kv_sample",
        )(x, mods, g.reshape(1, d), w_t)
    tl = _Tiling(x.shape, False, mods_rows)
    b, t, _ = x.shape
    tr = tl.rows
    n4 = 4 * N_KV * HEAD_DIM
    nw = min(WINDOW, t)
    assert tr == nw and tr % PAGE == 0
    ppt = tr // PAGE
    hm = pl.BlockSpec((1, N_KV, tr, LANES), lambda bi, i: (bi, 0, i, 0))
    hm_shape = jax.ShapeDtypeStruct((b, N_KV, t, LANES), BF16)
    return pl.pallas_call(
        _kv_prompt_kernel,
        grid=tl.grid,
        in_specs=[tl.x_spec(), tl.mod_spec(2, kblk), _const_spec((1, d)), _const_spec(w_t.shape),
                  _const_spec(w_perm.shape)],
        out_specs=[pl.BlockSpec((ppt, n4, PAGE), lambda bi, i: (bi * (t // tr) + i, 0, 0)),
                   pl.BlockSpec((1, nkv - n4, nw), lambda bi, i: (bi, 0, 0)), hm, hm, hm, hm],
        out_shape=[jax.ShapeDtypeStruct((b * t // PAGE, n4, PAGE), F32),
                   jax.ShapeDtypeStruct((b, nkv - n4, nw), F32), hm_shape, hm_shape, hm_shape, hm_shape],
        compiler_params=_cparams(tl.sem),
        name="kv_prompt",
    )(x, mods, g.reshape(1, d), w_t, w_perm)


def _compress_kernel(pt_ref, *refs):
    del pt_ref
    pages = refs[:PAGES_PER_STEP]
    w1_ref, w1f_ref, pe_ref, w2p_ref, w2n_ref = refs[PAGES_PER_STEP:PAGES_PER_STEP + 5]
    kcp_ref, vcp_ref, kcn_ref, vcn_ref = refs[PAGES_PER_STEP + 5:PAGES_PER_STEP + 9]
    x_ref, hs_ref = refs[PAGES_PER_STEP + 9:]
    nh = x_ref.shape[1]
    hid = w2p_ref.shape[1]
    per_page = PAGE // STRIDE
    pg0 = pl.program_id(1) * PAGES_PER_STEP

    pi = lax.broadcasted_iota(jnp.int32, (PAGE, PAGE), 0)
    pj = lax.broadcasted_iota(jnp.int32, (PAGE, PAGE), 1)
    perm = jnp.where(pj == (pi % per_page) * STRIDE + pi // per_page, 1.0, 0.0).astype(BF16)
    for pg in range(PAGES_PER_STEP):
        r0 = pl.multiple_of((pg0 + pg) * per_page, per_page)
        rp = _dot_nt(perm, pages[pg][0].astype(BF16))
        for l in range(STRIDE):
            x_ref[l, pl.ds(r0, per_page), :] = rp[l * per_page:(l + 1) * per_page, :]

    @pl.when(pl.program_id(1) == pl.num_programs(1) - 1)
    def _():
        rows = lax.broadcasted_iota(jnp.int32, (nh, hid), 0)
        hs_ref[nh:nh + 8, :] = jnp.zeros((8, hid), F32)
        for c, (pad_ref, nat_ref) in enumerate(((kcp_ref, kcn_ref), (vcp_ref, vcn_ref))):
            cvec = _dot(pe_ref[c], w1f_ref[c])[0:1, :]
            nat = jnp.zeros((nh, N_KV * HEAD_DIM), F32)
            for pr in range(N_KV // 2):
                lo = c * N_KV * HEAD_DIM + pr * LANES
                acc = jnp.zeros((nh, 4 * hid), F32)
                for l in range(STRIDE):
                    acc = acc + _dot(x_ref[l, :, lo:lo + LANES].astype(BF16), w1_ref[c, l])
                for e in range(2):
                    k = 2 * pr + e
                    hs_ref[0:nh, :] = acc[:, (2 * e + 1) * hid:(2 * e + 2) * hid]
                    hsum = acc[:, 2 * e * hid:(2 * e + 1) * hid] + hs_ref[pl.ds(1, nh), :] + cvec
                    hb = jnp.where(rows < nh - 1, _silu(hsum), 0.0).astype(BF16)
                    pad_ref[0, k] = _dot(hb, w2p_ref[c]).astype(BF16)
                    nat = nat + _dot(hb, w2n_ref[c, k])
            nat_ref[0] = nat.astype(BF16)


def _compress(pages, page_table, cw):
    bs, p = page_table.shape
    assert p % PAGES_PER_STEP == 0
    ng = p // PAGES_PER_STEP
    nh = p * (PAGE // STRIDE)
    wcols = 2 * N_KV * HEAD_DIM
    hid = cw["w2p"].shape[1]

    def page_spec(pg):
        return pl.BlockSpec((1, wcols, PAGE), lambda b, g, pt: (pt[b, g * PAGES_PER_STEP + pg], 0, 0))

    def full(a):
        nd = a.ndim
        return pl.BlockSpec(a.shape, lambda b, g, pt: (0,) * nd)

    consts = [cw["w1"], cw["w1f"], cw["pe"], cw["w2p"], cw["w2n"]]
    pad_shape = jax.ShapeDtypeStruct((bs, N_KV, nh, LANES), BF16)
    nat_shape = jax.ShapeDtypeStruct((bs, nh, N_KV * HEAD_DIM), BF16)
    pad_spec = pl.BlockSpec((1, N_KV, nh, LANES), lambda b, g, pt: (b, 0, 0, 0))
    nat_spec = pl.BlockSpec((1, nh, N_KV * HEAD_DIM), lambda b, g, pt: (b, 0, 0))
    grid_spec = pltpu.PrefetchScalarGridSpec(
        num_scalar_prefetch=1,
        grid=(bs, ng),
        in_specs=[page_spec(pg) for pg in range(PAGES_PER_STEP)] + [full(a) for a in consts],
        out_specs=[pad_spec, pad_spec, nat_spec, nat_spec],
        scratch_shapes=[pltpu.VMEM((STRIDE, nh, wcols), F32), pltpu.VMEM((nh + 8, hid), F32)],
    )
    return pl.pallas_call(
        _compress_kernel,
        grid_spec=grid_spec,
        out_shape=[pad_shape, pad_shape, nat_shape, nat_shape],
        compiler_params=_cparams(("arbitrary", "arbitrary")),
        name="compress",
    )(page_table, *([pages] * PAGES_PER_STEP), *consts)


def _qin_prompt_kernel(x_ref, m_ref, g_ref, w_ref, q_ref, gate_ref):
    tr, d = x_ref.shape[1], x_ref.shape[2]
    mod = _mod_getter(m_ref, False)
    h = _rms_mod(x_ref[...], g_ref[...][None], mod(0), mod(1)).reshape(tr, d).astype(BF16)
    a = _dot(h, w_ref[...])
    for hd in range(N_HEADS):
        q_ref[0, hd] = a[:, hd * LANES:(hd + 1) * LANES].astype(BF16)
    gate_ref[0] = a[:, N_HEADS * LANES:]


def _qin_sample_kernel(x_ref, m_ref, g_ref, w_ref, a_ref):
    tn, bk, d = x_ref.shape
    mod = _mod_getter(m_ref, True)
    h = _rms_mod(x_ref[...], g_ref[...][None], mod(0), mod(1)).reshape(tn * bk, d).astype(BF16)
    a_ref[...] = _dot(h, w_ref[...]).reshape(tn, bk, a_ref.shape[-1])


def _qin(x, mods, mods_rows, l, time_major, g1, w):
    tl = _Tiling(x.shape, time_major, mods_rows)
    d = tl.d
    n = w.shape[1]
    common = dict(grid=tl.grid, compiler_params=_cparams(tl.sem))
    in_specs = [tl.x_spec(), tl.mod_spec(6, l), _const_spec((1, d)), _const_spec(w.shape)]
    if time_major:
        return pl.pallas_call(
            _qin_sample_kernel, in_specs=in_specs,
            out_specs=pl.BlockSpec(tl.tile[:2] + (n,), tl.x_map),
            out_shape=jax.ShapeDtypeStruct(x.shape[:2] + (n,), F32),
            name="qin_sample", **common)(x, mods, g1.reshape(1, d), w)
    b, t, _ = x.shape
    tr = tl.rows
    return pl.pallas_call(
        _qin_prompt_kernel, in_specs=in_specs,
        out_specs=[pl.BlockSpec((1, N_HEADS, tr, LANES), lambda bi, i: (bi, 0, i, 0)),
                   pl.BlockSpec((1, tr, LANES), tl.x_map)],
        out_shape=[jax.ShapeDtypeStruct((b, N_HEADS, t, LANES), BF16), jax.ShapeDtypeStruct((b, t, LANES), F32)],
        name="qin_prompt", **common)(x, mods, g1.reshape(1, d), w)


def _block_rank(score_ref, ns):
    score = score_ref[...]
    blk = lax.broadcasted_iota(jnp.int32, score.shape, 0)
    rank = jnp.zeros(score.shape, F32)
    for i in range(ns):
        si = score_ref[i:i + 1, :]
        rank = rank + jnp.where(blk > i, jnp.where(si >= score, 1.0, 0.0), jnp.where(si > score, 1.0, 0.0))
    return rank


def _add_alibi(s, slopes_ref, head0, pos_f, qb):
    return jnp.concatenate([s[g * qb:(g + 1) * qb] + slopes_ref[head0 + g] * pos_f for g in range(GROUP)], axis=0)


def _attn_prompt_kernel(slopes_ref, q_ref, gate_ref, ks_ref, kw_ref, vs_ref, vw_ref, kc_ref, vc_ref, msel_ref,
                        ega_ref, egb_ref, o_ref,
                        s_ref, sc_ref, mt_ref, qa_ref, oc_ref, as_ref, mx_ref, ga_ref, gb_ref, ids_ref, cnt_ref):
    qb = q_ref.shape[2]
    rows = GROUP * qb
    t = ks_ref.shape[2]
    nc = kc_ref.shape[2]
    ns = msel_ref.shape[0]
    max_tiles = t // SEL_TILE
    blocks_per_tile = SEL_TILE // L_SEL
    i = pl.program_id(1)
    s0 = i * qb
    n_tiles = (s0 + qb + SEL_TILE - 1) // SEL_TILE
    qpos4 = s0 + lax.broadcasted_iota(jnp.int32, (rows, 1), 0) % qb
    low4 = lax.broadcasted_iota(jnp.int32, (rows, LANES), 1) < HEAD_DIM

    gs = jax.nn.sigmoid(gate_ref[0])
    g_hi = gs.astype(BF16)
    g_lo = (gs - g_hi.astype(F32)).astype(BF16)
    g2 = jnp.concatenate([g_hi, g_lo], axis=1)
    ga = _dot(g2, ega_ref[...])
    gb = _dot(g2, egb_ref[...])
    for h in range(N_HEADS):
        ga_ref[h] = ga[:, h * LANES:(h + 1) * LANES]
        gb_ref[h] = gb[:, h * LANES:(h + 1) * LANES]

    cidx = lax.broadcasted_iota(jnp.int32, (1, nc), 1)
    cend = cidx * STRIDE + (L_CMP - 1)
    cmask = (cend <= qpos4) & (cidx < nc - 1)
    cend_f = cend.astype(F32)
    row_ok = qpos4 >= L_CMP - 1
    blk = lax.broadcasted_iota(jnp.int32, (ns, qb), 0)
    qpos_t = s0 + lax.broadcasted_iota(jnp.int32, (ns, qb), 1)
    cur = qpos_t // L_SEL
    valid_t = blk * L_SEL <= qpos_t
    forced = (blk == 0) | (blk == cur) | (blk == cur - 1)
    tile_of_blk = jnp.where(lax.broadcasted_iota(jnp.int32, (max_tiles, ns), 1) // blocks_per_tile
                            == lax.broadcasted_iota(jnp.int32, (max_tiles, ns), 0), 1.0, 0.0).astype(BF16)

    for kvh in range(N_KV):
        q4 = q_ref[0, kvh * GROUP:(kvh + 1) * GROUP].reshape(rows, LANES)
        sc = _add_alibi(_dot_nt(q4, kc_ref[0, kvh]), slopes_ref, kvh * GROUP, cend_f, qb)
        sc = jnp.where(cmask, sc, NEG)
        e = jnp.exp(sc - jnp.max(sc, axis=1, keepdims=True))
        r = jnp.where(row_ok, 1.0 / jnp.sum(e, axis=1, keepdims=True), 0.0)
        pb = (e * r).astype(BF16)
        oc_ref[kvh] = _dot(pb, vc_ref[0, kvh])
        imp4 = _dot_nt(msel_ref[...], pb)
        imp_t = imp4[:, 0:qb]
        for g in range(1, GROUP):
            imp_t = imp_t + imp4[:, g * qb:(g + 1) * qb]
        sc_ref[kvh] = jnp.where(valid_t, imp_t + jnp.where(forced, FORCE, 0.0), -jnp.inf)
        keep = (_block_rank(sc_ref.at[kvh], ns) < TOP_N) & valid_t
        mt_ref[kvh] = jnp.zeros(mt_ref.shape[1:], F32)
        mt_ref[kvh, HEAD_DIM:HEAD_DIM + ns, :] = jnp.where(keep, 0.0, NEG)
        mask_b = mt_ref[kvh].T.astype(BF16)
        qa_ref[kvh] = jnp.where(low4, q4, jnp.concatenate([mask_b] * GROUP, axis=0))

        per_tile = jnp.sum(_dot(tile_of_blk, jnp.where(keep, 1.0, 0.0).astype(BF16)), axis=1, keepdims=True)
        n_act = jnp.int32(0)
        spare = jnp.int32(0)
        for pos in range(max_tiles):
            ids_ref[kvh, pos] = jnp.int32(0)
        for kt in range(max_tiles - 1):
            in_range = kt < n_tiles - 1
            active = jnp.logical_and(in_range, per_tile[kt, 0] > 0.5)
            ids_ref[kvh, n_act] = jnp.int32(kt)
            n_act = n_act + active.astype(jnp.int32)
            spare = jnp.where(jnp.logical_and(in_range, jnp.logical_not(active)), jnp.int32(kt), spare)
        cnt_ref[kvh] = n_act
        cnt_ref[N_KV + kvh] = spare

    lane_blocks = SEL_TILE // LANES
    for pair in range(N_KV // 2):
        kvs = (2 * pair, 2 * pair + 1)
        n_steps = jnp.maximum(cnt_ref[kvs[0]], cnt_ref[kvs[1]])
        for kvh in kvs:
            mx_ref[kvh] = jnp.full((rows, LANES), NEG, F32)
            as_ref[kvh] = jnp.zeros((rows, LANES), F32)
            for pos in range(max_tiles - 1):
                ids_ref[kvh, pos] = jnp.where(pos >= cnt_ref[kvh], cnt_ref[N_KV + kvh], ids_ref[kvh, pos])

        def scores(kts, masked):
            for j, kvh in enumerate(kvs):
                k0 = pl.multiple_of(kts[j] * SEL_TILE, SEL_TILE)
                kpos = k0 + lax.broadcasted_iota(jnp.int32, (1, SEL_TILE), 1)
                s = _dot_nt(qa_ref[kvh], ks_ref[0, kvh, pl.ds(k0, SEL_TILE), :])
                s = _add_alibi(s, slopes_ref, kvh * GROUP, kpos.astype(F32), qb)
                if masked:
                    s = jnp.where(kpos <= qpos4, s, NEG)
                s_ref[j, kts[j]] = s
                m = s[:, 0:LANES]
                for c in range(1, lane_blocks):
                    m = jnp.maximum(m, s[:, c * LANES:(c + 1) * LANES])
                mx_ref[kvh] = jnp.maximum(mx_ref[kvh], m)

        def weighted(kts, mrow):
            for j, kvh in enumerate(kvs):
                k0 = pl.multiple_of(kts[j] * SEL_TILE, SEL_TILE)
                p = jnp.exp(s_ref[j, kts[j]] - mrow[j]).astype(BF16)
                as_ref[kvh] += _dot(p, vs_ref[0, kvh, pl.ds(k0, SEL_TILE), :])

        def listed(step):
            return tuple(ids_ref[kvh, step] for kvh in kvs)

        def score_step(step, carry):
            scores(listed(step), False)
            return carry

        lax.fori_loop(0, n_steps, score_step, 0)
        scores((n_tiles - 1,) * 2, True)
        mrow = [jnp.max(mx_ref[kvh], axis=1, keepdims=True) for kvh in kvs]

        def weight_step(step, carry):
            weighted(listed(step), mrow)
            return carry

        lax.fori_loop(0, n_steps, weight_step, 0)
        weighted((n_tiles - 1,) * 2, mrow)

    wlen = WINDOW + qb
    w0 = pl.multiple_of(jnp.maximum(s0 - WINDOW, 0), qb)
    wpos = w0 + lax.broadcasted_iota(jnp.int32, (1, wlen), 1)
    wpos_f = wpos.astype(F32)
    dist = qpos4 - wpos
    wmask = (dist >= 0) & (dist < WINDOW)
    for kvh in range(N_KV):
        q4 = q_ref[0, kvh * GROUP:(kvh + 1) * GROUP].reshape(rows, LANES)
        s = _add_alibi(_dot_nt(q4, kw_ref[0, kvh, pl.ds(w0, wlen), :]), slopes_ref, kvh * GROUP, wpos_f, qb)
        s = jnp.where(wmask, s, NEG)
        p = jnp.exp(s - jnp.max(s, axis=1, keepdims=True)).astype(BF16)
        acc_w = _dot(p, vw_ref[0, kvh, pl.ds(w0, wlen), :])
        acc_s = as_ref[kvh]
        o_s = acc_s * (1.0 / pltpu.roll(acc_s, HEAD_DIM, 1))
        o_w = acc_w * (1.0 / pltpu.roll(acc_w, HEAD_DIM, 1))
        ga4 = ga_ref[kvh * GROUP:(kvh + 1) * GROUP].reshape(rows, LANES)
        gb4 = gb_ref[kvh * GROUP:(kvh + 1) * GROUP].reshape(rows, LANES)
        out = ga4 * jnp.where(low4, oc_ref[kvh], o_w) + gb4 * jnp.where(low4, o_s, 0.0)
        o_ref[0, kvh * GROUP:(kvh + 1) * GROUP] = out.reshape(GROUP, qb, LANES).astype(BF16)


def _gate_expand():
    src = jnp.arange(2 * LANES) % LANES
    br = (src // N_HEADS)[:, None]
    hd = (src % N_HEADS)[:, None]
    col = jnp.arange(N_HEADS * LANES)
    same = hd == (col // LANES)[None, :]
    low = (col % LANES < HEAD_DIM)[None, :]
    ea = same & (((br == 0) & low) | ((br == 2) & ~low))
    eb = same & (br == 1) & low
    return ea.astype(BF16), eb.astype(BF16)


def _attn_prompt(slopes, q, gates, ks, kw, vs, vw, kcp, vcp, msel_t):
    b, _, t, _ = q.shape
    nc = kcp.shape[2]
    ns = msel_t.shape[0]
    assert t % SEL_TILE == 0 and t >= WINDOW + Q_BLK and ns <= HEAD_DIM and ns % 8 == 0
    ea, eb = _gate_expand()
    rows = GROUP * Q_BLK
    max_tiles = t // SEL_TILE
    qspec = pl.BlockSpec((1, N_HEADS, Q_BLK, LANES), lambda bi, i: (bi, 0, i, 0))
    kvspec = pl.BlockSpec((1, N_KV, t, LANES), lambda bi, i: (bi, 0, 0, 0), pipeline_mode=pl.Buffered(1))
    cspec = pl.BlockSpec((1, N_KV, nc, LANES), lambda bi, i: (bi, 0, 0, 0))
    return pl.pallas_call(
        _attn_prompt_kernel,
        grid=(b, t // Q_BLK),
        in_specs=[pl.BlockSpec(memory_space=pltpu.SMEM), qspec,
                  pl.BlockSpec((1, Q_BLK, LANES), lambda bi, i: (bi, i, 0)),
                  kvspec, kvspec, kvspec, kvspec, cspec, cspec,
                  _const_spec(msel_t.shape), _const_spec(ea.shape), _const_spec(eb.shape)],
        out_specs=qspec,
        out_shape=jax.ShapeDtypeStruct(q.shape, BF16),
        scratch_shapes=[pltpu.VMEM((2, max_tiles, rows, SEL_TILE), F32),
                        pltpu.VMEM((N_KV, ns, Q_BLK), F32), pltpu.VMEM((N_KV, LANES, Q_BLK), F32),
                        pltpu.VMEM((N_KV, rows, LANES), BF16), pltpu.VMEM((N_KV, rows, LANES), F32),
                        pltpu.VMEM((N_KV, rows, LANES), F32), pltpu.VMEM((N_KV, rows, LANES), F32),
                        pltpu.VMEM((N_HEADS, Q_BLK, LANES), F32), pltpu.VMEM((N_HEADS, Q_BLK, LANES), F32),
                        pltpu.SMEM((N_KV, max_tiles), jnp.int32), pltpu.SMEM((2 * N_KV,), jnp.int32)],
        compiler_params=_cparams(("arbitrary", "arbitrary")),
        name="attn_prompt",
    )(slopes, q, gates, ks, kw, vs, vw, kcp, vcp, msel_t, ea, eb)


def _attn_sample_kernel(pt_ref, *refs, past):
    del pt_ref
    npg = past // PAGE
    pages = refs[:npg]
    (q_ref, gate_ref, kc_ref, vc_ref, sw_ref, kvn_ref, slope_ref, msel_ref, o_ref,
     s_ref, sc_ref, st_ref) = refs[npg:]
    tn = q_ref.shape[1]
    hw = N_KV * HEAD_DIM
    rows = N_KV * GROUP * tn
    nc = kc_ref.shape[1]
    ns_pad = sc_ref.shape[0]
    nblk = npg + 1
    n_s = (past + tn + L_SEL - 1) // L_SEL
    f_sel = 2 * hw
    slope = slope_ref[...]

    rr = lax.broadcasted_iota(jnp.int32, (rows, 1), 0)
    qpos = past + rr % tn
    col = lax.broadcasted_iota(jnp.int32, (tn, hw), 1) // HEAD_DIM
    colr = lax.broadcasted_iota(jnp.int32, (rows, hw), 1) // HEAD_DIM
    rowk = lax.broadcasted_iota(jnp.int32, (rows, hw), 0) // (GROUP * tn)
    diag = colr == rowk

    qs = q_ref[0]
    pieces = []
    for k in range(N_KV):
        for g in range(GROUP):
            pieces.append(jnp.where(col == k, qs[:, g * hw:(g + 1) * hw], 0.0))
    qbd = jnp.concatenate(pieces, axis=0).astype(BF16)

    cidx = lax.broadcasted_iota(jnp.int32, (1, nc), 1)
    cend = cidx * STRIDE + (L_CMP - 1)
    cvalid = (cend <= qpos) & (cidx < nc - 1)
    sc = jnp.where(cvalid, _dot_nt(qbd, kc_ref[0]) + slope * cend.astype(F32), NEG)
    e = jnp.exp(sc - jnp.max(sc, axis=1, keepdims=True))
    r = jnp.where(qpos >= L_CMP - 1, 1.0 / jnp.sum(e, axis=1, keepdims=True), 0.0)
    pc = (e * r).astype(BF16)
    o_c = _dot(pc, vc_ref[0])

    imp = _dot_nt(msel_ref[...], pc)
    tot = imp
    for g in range(1, GROUP):
        tot = tot + pltpu.roll(imp, rows - g * tn, 1)
    blk = lax.broadcasted_iota(jnp.int32, (ns_pad, rows), 0)
    lane_r = lax.broadcasted_iota(jnp.int32, (ns_pad, rows), 1)
    qpos_l = past + lane_r % tn
    cur = qpos_l // L_SEL
    valid_l = (blk * L_SEL <= qpos_l) & (blk < n_s)
    forced = (blk == 0) | (blk == cur) | (blk == cur - 1)
    sc_ref[...] = jnp.where(valid_l, tot + jnp.where(forced, FORCE, 0.0), -jnp.inf)
    keep = (_block_rank(sc_ref, n_s) < TOP_N) & valid_l
    lead = (lane_r // tn) % GROUP == 0
    keep_f = jnp.where(keep & lead, 1.0, 0.0)
    spread = keep_f
    for g in range(1, GROUP):
        spread = spread + pltpu.roll(keep_f, g * tn, 1)
    st_ref[...] = jnp.zeros(st_ref.shape, F32)
    st_ref[0:ns_pad, :] = spread
    keep_rows = st_ref[...].T.astype(BF16)

    zpad = jnp.zeros((PAGE - tn, hw), F32)
    bsel = lax.broadcasted_iota(jnp.int32, (LANES, PAGE), 0)
    ksub = lax.broadcasted_iota(jnp.int32, (LANES, PAGE), 1) // L_SEL
    kloc = lax.broadcasted_iota(jnp.int32, (1, PAGE), 1)
    mrun = jnp.full((rows, 1), NEG, F32)
    for j in range(nblk):
        if j < npg:
            qk = _dot(qbd, pages[j][0, 0:hw, :].astype(BF16))
        else:
            kb = jnp.concatenate([kvn_ref[0, :, f_sel:f_sel + hw], zpad], axis=0).astype(BF16)
            qk = _dot_nt(qbd, kb)
        kpos = j * PAGE + kloc
        expand = jnp.where(bsel == (PAGE // L_SEL) * j + ksub, 1.0, 0.0).astype(BF16)
        kept = _dot(keep_rows, expand)
        s = jnp.where((kept > 0.5) & (kpos <= qpos), qk + slope * kpos.astype(F32), NEG)
        s_ref[:, j * PAGE:(j + 1) * PAGE] = s
        mrun = jnp.maximum(mrun, jnp.max(s, axis=1, keepdims=True))
    lsum = jnp.zeros((rows, 1), F32)
    o_s = jnp.zeros((rows, hw), F32)
    for j in range(nblk):
        p = jnp.exp(s_ref[:, j * PAGE:(j + 1) * PAGE] - mrun)
        lsum = lsum + jnp.sum(p, axis=1, keepdims=True)
        if j < npg:
            o_s = o_s + _dot_nt(p.astype(BF16), pages[j][0, hw:2 * hw, :].astype(BF16))
        else:
            vb = jnp.concatenate([kvn_ref[0, :, f_sel + hw:f_sel + 2 * hw], zpad], axis=0).astype(BF16)
            o_s = o_s + _dot(p.astype(BF16), vb)
    o_s = o_s * (1.0 / lsum)

    nbuf = sw_ref.shape[2]
    f_win = 4 * hw
    kwn = jnp.concatenate([kvn_ref[0, :, f_win:f_win + hw], zpad], axis=0).astype(BF16)
    vwn = jnp.concatenate([kvn_ref[0, :, f_win + hw:f_win + 2 * hw], zpad], axis=0).astype(BF16)
    pos_a = past - nbuf + lax.broadcasted_iota(jnp.int32, (1, nbuf), 1)
    pos_b = past + kloc
    da = qpos - pos_a
    db = qpos - pos_b
    va = (da >= 0) & (da < WINDOW) & (pos_a >= 0)
    vb_ok = (db >= 0) & (db < WINDOW)
    s_a = jnp.where(va, _dot(qbd, sw_ref[0, 0:hw, :].astype(BF16)) + slope * pos_a.astype(F32), NEG)
    s_b = jnp.where(vb_ok, _dot_nt(qbd, kwn) + slope * pos_b.astype(F32), NEG)
    m = jnp.maximum(jnp.max(s_a, axis=1, keepdims=True), jnp.max(s_b, axis=1, keepdims=True))
    p_a = jnp.exp(s_a - m)
    p_b = jnp.exp(s_b - m)
    l_w = jnp.sum(p_a, axis=1, keepdims=True) + jnp.sum(p_b, axis=1, keepdims=True)
    o_w = (_dot_nt(p_a.astype(BF16), sw_ref[0, hw:2 * hw, :].astype(BF16)) + _dot(p_b.astype(BF16), vwn)) * (1.0 / l_w)

    gt = jax.nn.sigmoid(gate_ref[0])
    tot_o = jnp.where(diag, gt[0] * o_c + gt[1] * o_s + gt[2] * o_w, 0.0)
    for g in range(GROUP):
        acc = jnp.zeros((tn, hw), F32)
        for k in range(N_KV):
            r0 = (k * GROUP + g) * tn
            acc = acc + tot_o[r0:r0 + tn, :]
        o_ref[0, :, g * hw:(g + 1) * hw] = acc


def _attn_sample(page_table, cache_t, q, gates, kcn, vcn, state_win_t, kv_new, slope_rows, msel_t, past):
    bd, tn, qw = q.shape
    npg = past // PAGE
    hw = N_KV * HEAD_DIM
    rows = N_KV * GROUP * tn
    assert rows == LANES and msel_t.shape[1] == kcn.shape[1]
    ns_pad = msel_t.shape[0]

    def page_spec(pg):
        return pl.BlockSpec((1, 2 * hw, PAGE), lambda b, pt: (pt[b, pg], 1, 0))

    def per_seq(a):
        nd = a.ndim
        return pl.BlockSpec((1,) + a.shape[1:], lambda b, pt: (b,) + (0,) * (nd - 1))

    def full(a):
        nd = a.ndim
        return pl.BlockSpec(a.shape, lambda b, pt: (0,) * nd)

    grid_spec = pltpu.PrefetchScalarGridSpec(
        num_scalar_prefetch=1,
        grid=(bd,),
        in_specs=[page_spec(pg) for pg in range(npg)]
        + [per_seq(q), per_seq(gates), per_seq(kcn), per_seq(vcn), per_seq(state_win_t), per_seq(kv_new),
           full(slope_rows), full(msel_t)],
        out_specs=pl.BlockSpec((1, tn, N_HEADS * HEAD_DIM), lambda b, pt: (b, 0, 0)),
        scratch_shapes=[pltpu.VMEM((rows, (npg + 1) * PAGE), F32), pltpu.VMEM((ns_pad, rows), F32),
                        pltpu.VMEM((LANES, rows), F32)],
    )
    return pl.pallas_call(
        functools.partial(_attn_sample_kernel, past=past),
        grid_spec=grid_spec,
        out_shape=jax.ShapeDtypeStruct((bd, tn, N_HEADS * HEAD_DIM), F32),
        compiler_params=_cparams(("arbitrary",)),
        name="attn_sample",
    )(page_table, *([cache_t] * npg), q, gates, kcn, vcn, state_win_t, kv_new, slope_rows, msel_t)


def _msel_t(n_c_rows, n_s_rows):
    i0 = jnp.arange(n_c_rows)[None, :] * STRIDE
    j0 = jnp.arange(n_s_rows)[:, None] * L_SEL
    return ((i0 < j0 + L_SEL) & (i0 + L_CMP > j0)).astype(BF16)


def _prep_ffn(w_up, w_down):
    d, two_ff = w_up.shape
    ff = two_ff // 2
    n = ff // FF_CHUNK
    assert n * FF_CHUNK == ff
    a = w_up[:, :ff].reshape(d, n, FF_CHUNK)
    b = w_up[:, ff:].reshape(d, n, FF_CHUNK)
    wup_c = jnp.concatenate([a, b], axis=2).transpose(1, 0, 2).astype(BF16)
    wdn_c = w_down.reshape(n, FF_CHUNK, w_down.shape[1]).astype(BF16)
    return wup_c, wdn_c


def _prep_compress(w_cmp1, w_cmp2, pe_cmp):
    hid = w_cmp1.shape[-1]
    wl = jnp.concatenate([w_cmp1[:, :STRIDE], w_cmp1[:, STRIDE:]], axis=-1)
    z = jnp.zeros_like(wl)
    w1 = jnp.concatenate([jnp.concatenate([wl, z], axis=-1), jnp.concatenate([z, wl], axis=-1)], axis=2)
    w1f = w_cmp1.reshape(2, L_CMP * HEAD_DIM, hid)
    pe = jnp.zeros((2, 8, L_CMP * HEAD_DIM), F32).at[:, 0].set(pe_cmp.reshape(2, -1))
    w2p = jnp.concatenate([w_cmp2, jnp.zeros_like(w_cmp2)], axis=-1)
    w2n = jnp.zeros((2, N_KV, hid, N_KV * HEAD_DIM), F32)
    for k in range(N_KV):
        w2n = w2n.at[:, k, :, k * HEAD_DIM:(k + 1) * HEAD_DIM].set(w_cmp2)
    return dict(w1=w1.astype(BF16), w1f=w1f.astype(BF16), pe=pe.astype(BF16), w2p=w2p.astype(BF16),
                w2n=w2n.astype(BF16))


def _prep_kv_perm(w_kv):
    d = w_kv.shape[0]
    w6 = w_kv.reshape(d, 6, N_KV, HEAD_DIM)
    z = jnp.zeros((d, N_KV, HEAD_DIM), w_kv.dtype)
    ks = jnp.concatenate([w6[:, 2], z], axis=-1)
    kw = jnp.concatenate([w6[:, 4], z], axis=-1)
    vs = jnp.concatenate([w6[:, 3], z], axis=-1)
    vw = jnp.concatenate([z, w6[:, 5]], axis=-1)
    return jnp.concatenate([ks, kw, vs, vw], axis=1).reshape(d, 4 * N_KV * LANES).astype(BF16)


def _prep_attn_prompt(w_in, w_o):
    d = w_in.shape[0]
    qw = N_HEADS * HEAD_DIM
    scale = HEAD_DIM ** -0.5
    wq = (w_in[:, :qw] * scale).reshape(d, N_HEADS, HEAD_DIM)
    wq = jnp.concatenate([wq, jnp.zeros_like(wq)], axis=-1).reshape(d, N_HEADS * LANES)
    wg = w_in[:, qw:].reshape(d, N_HEADS, 3).transpose(0, 2, 1).reshape(d, 3 * N_HEADS)
    wg = jnp.concatenate([wg, jnp.zeros((d, LANES - 3 * N_HEADS), w_in.dtype)], axis=1)
    w_in_p = jnp.concatenate([wq, wg], axis=1).astype(BF16)
    wo3 = w_o.reshape(N_HEADS, HEAD_DIM, w_o.shape[1])
    wo_p = jnp.concatenate([wo3, wo3], axis=1).reshape(N_HEADS * LANES, w_o.shape[1]).astype(BF16)
    return w_in_p, wo_p


def _prep_attn_sample(w_in, w_o):
    d = w_in.shape[0]
    qw = N_HEADS * HEAD_DIM
    scale = HEAD_DIM ** -0.5
    wq = (w_in[:, :qw] * scale).reshape(d, N_KV, GROUP, HEAD_DIM).transpose(0, 2, 1, 3).reshape(d, qw)
    w_in_s = jnp.concatenate([wq, w_in[:, qw:]], axis=1).astype(BF16)
    wo_s = w_o.reshape(N_KV, GROUP, HEAD_DIM, w_o.shape[1]).transpose(1, 0, 2, 3).reshape(qw, w_o.shape[1])
    return w_in_s, wo_s.astype(BF16)


def kernel(x_prompt, x_sample, cache_kv, page_table, state_win, state_conv, c_prompt, c_sample, w_ada, b_ada, norm_g, conv_w_pw1, conv_b_pw1, conv_w_dw, conv_b_dw, conv_ln_g, conv_ln_b, conv_w_pw2, ffn_w_up, ffn_w_down, kv_norm_g, w_kv, w_cmp1, w_cmp2, pe_cmp, nsa_w_in, nsa_w_o, final_norm_g):
    b, t, d = x_prompt.shape
    bd, tn, _ = x_sample.shape
    depth = norm_g.shape[0]
    n_a = conv_w_pw1.shape[0]
    n_mod = b_ada.shape[0] // d
    past = page_table.shape[1] * PAGE
    hw = N_KV * HEAD_DIM
    assert bd % 8 == 0 and b <= 8 and n_mod == 6 * depth + 4

    c_all = jnp.concatenate([c_sample, c_prompt, jnp.zeros((8 - b % 8, d), F32)], axis=0)
    mods = _mods(c_all, w_ada, b_ada, n_mod)

    ffn_w = [_prep_ffn(ffn_w_up[l], ffn_w_down[l]) for l in range(depth)]
    w_pw1 = conv_w_pw1.astype(BF16)
    w_pw2 = conv_w_pw2.astype(BF16)
    w_kv_t = w_kv.T.astype(BF16)
    w_kv_perm = _prep_kv_perm(w_kv)
    cw = _prep_compress(w_cmp1, w_cmp2, pe_cmp)
    slopes = jnp.exp2(-8.0 * jnp.arange(1, N_HEADS + 1, dtype=F32) / N_HEADS)
    kblk_kv = (6 * depth) // 2
    kblk_final = (6 * depth + 2) // 2

    def trunk(x, time_major, conv_prev_tm, attn_fn_builder):
        conv_states = []
        attn_fn = None
        extra = None
        for l in range(depth):
            if l == n_a:
                attn_fn, extra = attn_fn_builder(x)
            if l < n_a:
                x, st = _conv_layer(x, None if conv_prev_tm is None else conv_prev_tm[l], mods, bd, l, time_major,
                                    norm_g[l, 0], w_pw1[l], conv_b_pw1[l], conv_w_dw[l], conv_b_dw[l],
                                    conv_ln_g[l], conv_ln_b[l], w_pw2[l])
                conv_states.append(st)
                o = wo = None
            else:
                o, wo = attn_fn(x, l)
            last = l == depth - 1
            x = _ffn_layer(x, mods, bd, l, time_major, norm_g[l, 1], ffn_w[l][0], ffn_w[l][1], o=o, wo=wo,
                           final_g=final_norm_g if last else None, n_mod_final=kblk_final)
        return x, extra, conv_states

    def from_feature_major(a, lead):
        nl = len(lead)
        a = a.reshape(lead + (-1, N_KV, HEAD_DIM, a.shape[-1]))
        return a.transpose(tuple(range(nl)) + (nl + 3, nl, nl + 1, nl + 2))

    attn_w_p = [_prep_attn_prompt(nsa_w_in[i], nsa_w_o[i]) for i in range(depth - n_a)]

    def prompt_attn_builder(x):
        kvt_pages, win_t, ks, kw, vs, vw = _kv_proj(x, mods, bd, kblk_kv, False, kv_norm_g, w_kv_t, w_kv_perm)
        ident = jnp.arange(b * t // PAGE, dtype=jnp.int32).reshape(b, t // PAGE)
        kcp, vcp, _, _ = _compress(kvt_pages, ident, cw)
        msel_t = _msel_t(t // STRIDE, t // L_SEL)

        def attn(xl, l):
            w_in_p, wo_p = attn_w_p[l - n_a]
            q, gates = _qin(xl, mods, bd, l, False, norm_g[l, 0], w_in_p)
            return _attn_prompt(slopes, q, gates, ks, kw, vs, vw, kcp, vcp, msel_t), wo_p

        extra = (from_feature_major(kvt_pages, (b * t // PAGE,)), from_feature_major(win_t, (b,)))
        return attn, extra

    y_prompt, (kv_prompt, win_prompt), conv_p = trunk(x_prompt, False, None, prompt_attn_builder)
    conv_prompt = jnp.stack([st[:, 32 - (CONV_W - 1):] for st in conv_p])

    attn_w_s = [_prep_attn_sample(nsa_w_in[i], nsa_w_o[i]) for i in range(depth - n_a)]
    cache_t = cache_kv.transpose(0, 2, 3, 4, 1).reshape(cache_kv.shape[0], 4 * hw, PAGE)
    nbuf = state_win.shape[1]
    state_win_t = state_win.transpose(0, 2, 3, 4, 1).reshape(bd, 2 * hw, nbuf)
    slope_rows = jnp.repeat(slopes, tn).reshape(N_HEADS * tn, 1)

    def sample_attn_builder(x):
        kvt = _kv_proj(x, mods, bd, kblk_kv, True, kv_norm_g, w_kv_t, None)
        kvt3 = kvt.reshape(6 * hw, tn, bd)
        kv_new = kvt3.transpose(2, 1, 0)
        _, _, kcn, vcn = _compress(cache_t, page_table, cw)
        n_s = -(-(past + tn) // L_SEL)
        msel_t = _msel_t(kcn.shape[1], -(-n_s // 8) * 8)

        def attn(xl, l):
            w_in_s, wo_s = attn_w_s[l - n_a]
            a = _qin(xl, mods, bd, l, True, norm_g[l, 0], w_in_s).transpose(1, 0, 2)
            q = a[..., :N_HEADS * HEAD_DIM]
            gates = a[..., N_HEADS * HEAD_DIM:].reshape(bd, tn, N_HEADS, 3).transpose(0, 3, 2, 1)
            gates = gates.reshape(bd, 3, N_HEADS * tn, 1)
            o = _attn_sample(page_table, cache_t, q, gates, kcn, vcn, state_win_t, kv_new, slope_rows, msel_t,
                             past)
            return o.transpose(1, 0, 2), wo_s

        n_keep = min(WINDOW, past + tn)
        win_all_t = jnp.concatenate([state_win_t, kvt3[4 * hw:].transpose(2, 0, 1)], axis=2)[:, :, -n_keep:]
        kv_sample = kvt3[:4 * hw].reshape(4, N_KV, HEAD_DIM, tn, bd).transpose(4, 3, 0, 1, 2)
        return attn, (kv_sample, from_feature_major(win_all_t, (bd,)))

    conv_prev_tm = state_conv.transpose(0, 2, 1, 3)
    y_s_tm, (kv_sample, win_sample), conv_s = trunk(x_sample.transpose(1, 0, 2), True, conv_prev_tm,
                                                    sample_attn_builder)
    y_sample = y_s_tm.transpose(1, 0, 2)
    conv_sample = jnp.stack(conv_s).transpose(0, 2, 1, 3)

    return (y_prompt, y_sample, kv_prompt, kv_sample, win_prompt, win_sample, conv_prompt, conv_sample)
```

```python
import functools

import jax
import jax.numpy as jnp
from jax import lax
from jax.experimental import pallas as pl
from jax.experimental.pallas import tpu as pltpu

F32 = jnp.float32
BF16 = jnp.bfloat16

N_HEADS = 16
N_KV = 4
GROUP = N_HEADS // N_KV
HEAD_DIM = 64
L_CMP = 32
STRIDE = 16
L_SEL = 64
TOP_N = 16
WINDOW = 512
Q_BLK = 128
PAGE = 128
CONV_W = 31
EPS = 1e-6
FORCE = 1e3
NEG = -1e30

LANES = 128
SUBLANES = 8
FF_CHUNK = 256
SEL_TILE = 256
PAGES_PER_STEP = 16
VMEM_LIMIT = 56 * 1024 * 1024


def _cparams(sem):
    return pltpu.CompilerParams(dimension_semantics=sem, vmem_limit_bytes=VMEM_LIMIT)


def _const_spec(shape):
    nd = len(shape)
    return pl.BlockSpec(shape, lambda *_: (0,) * nd, pipeline_mode=pl.Buffered(1))


def _dot(a, b):
    return jnp.dot(a, b, preferred_element_type=F32)


def _dot_nt(a, b):
    return lax.dot_general(a, b, (((1,), (1,)), ((), ())), preferred_element_type=F32)


def _silu(x):
    return x * jax.nn.sigmoid(x)


def _rms_mod(x, g, shift, scale):
    ms = jnp.mean(x * x, axis=-1, keepdims=True)
    y = x * lax.rsqrt(ms + EPS) * g
    return y * (1.0 + scale) + shift


def _mod_getter(m_ref, time_major):
    if time_major:
        return lambda k: m_ref[k][None]
    b = pl.program_id(0)
    return lambda k: m_ref[k, pl.ds(b, 1), :][None]


def _mods_kernel(c_ref, w_ref, b_ref, o_ref):
    c = c_ref[...]
    s = _silu(c).astype(BF16)
    o_ref[0] = _dot(s, w_ref[...].astype(BF16)) + b_ref[0]


def _mods(c_all, w_ada, b_ada, n_mod):
    r, d = c_all.shape
    return pl.pallas_call(
        _mods_kernel,
        grid=(n_mod,),
        in_specs=[pl.BlockSpec((r, d), lambda k: (0, 0)),
                  pl.BlockSpec((d, d), lambda k: (0, k)),
                  pl.BlockSpec((1, 1, d), lambda k: (k, 0, 0))],
        out_specs=pl.BlockSpec((1, r, d), lambda k: (k, 0, 0)),
        out_shape=jax.ShapeDtypeStruct((n_mod, r, d), F32),
        compiler_params=_cparams(("arbitrary",)),
        name="mods",
    )(c_all, w_ada, b_ada.reshape(n_mod, 1, d))


class _Tiling:
    def __init__(self, x_shape, time_major, mods_rows, bk=None):
        self.time_major = time_major
        if time_major:
            tn, bd, d = x_shape
            bk = min(bd, 32) if bk is None else bk
            assert bd % bk == 0 and bk % 8 == 0
            self.grid = (bd // bk,)
            self.tile = (tn, bk, d)
            self.rows = tn * bk
            self.x_map = lambda j: (0, j, 0)
            self.mod_block = lambda k: (k, bk, d)
            self.mod_map = lambda kblk: (lambda j: (kblk, j, 0))
            self.sem = ("arbitrary",)
        else:
            b, t, d = x_shape
            tr = min(t, 512)
            assert t % tr == 0
            self.grid = (b, t // tr)
            self.tile = (1, tr, d)
            self.rows = tr
            self.x_map = lambda bi, i: (bi, i, 0)
            self.mod_block = lambda k: (k, 8, d)
            self.mod_map = lambda kblk: (lambda bi, i: (kblk, mods_rows // 8, 0))
            self.sem = ("arbitrary", "arbitrary")
        self.d = d

    def x_spec(self):
        return pl.BlockSpec(self.tile, self.x_map)

    def mod_spec(self, k, kblk):
        return pl.BlockSpec(self.mod_block(k), self.mod_map(kblk))


def _conv_core(y, ln_g, ln_b):
    mu = jnp.mean(y, axis=-1, keepdims=True)
    yc = y - mu
    var = jnp.mean(yc * yc, axis=-1, keepdims=True)
    return _silu(yc * lax.rsqrt(var + EPS) * ln_g + ln_b)


def _glu_rows(h2, w1_ref, b1_ref, d):
    a1 = _dot(h2, w1_ref[:, :d]) + b1_ref[:, :d]
    a2 = _dot(h2, w1_ref[:, d:]) + b1_ref[:, d:]
    return a1 * jax.nn.sigmoid(a2)


def _conv_prompt_kernel(x_ref, m_ref, g_ref, w1_ref, b1_ref, wdw_ref, bdw_ref, lng_ref, lnb_ref, w2_ref,
                        xo_ref, st_ref, s_ref, p_ref):
    tr, d = x_ref.shape[1], x_ref.shape[2]
    halo = 32
    off = halo - (CONV_W - 1)
    mod = _mod_getter(m_ref, False)

    @pl.when(pl.program_id(1) == 0)
    def _():
        s_ref[0:halo, :] = jnp.zeros((halo, d), F32)
        s_ref[halo + tr:halo + tr + SUBLANES, :] = jnp.zeros((SUBLANES, d), F32)

    x = x_ref[...]
    h = _rms_mod(x, g_ref[...][None], mod(0), mod(1))
    s_ref[halo:halo + tr, :] = _glu_rows(h.reshape(tr, d).astype(BF16), w1_ref, b1_ref, d)

    acc = jnp.broadcast_to(bdw_ref[...], (tr, d))
    for a in range(SUBLANES):
        part = None
        for r in range(a, off + CONV_W, SUBLANES):
            if r >= off:
                term = wdw_ref[r - off:r - off + 1, :] * s_ref[r - a:r - a + tr + SUBLANES, :]
                part = term if part is None else part + term
        if a == 0:
            acc = acc + part[0:tr]
        else:
            p_ref[...] = part
            acc = acc + p_ref[pl.ds(a, tr), :]
    z = _conv_core(acc, lng_ref[...], lnb_ref[...]).astype(BF16)
    y2 = _dot(z, w2_ref[...])
    xo_ref[...] = x + mod(2) * y2[None]
    tail = s_ref[tr:tr + halo, :]
    st_ref[0] = tail
    s_ref[0:halo, :] = tail


def _conv_sample_kernel(x_ref, p_ref, m_ref, g_ref, w1_ref, b1_ref, wdw_ref, bdw_ref, lng_ref, lnb_ref, w2_ref,
                        xo_ref, st_ref, s_ref):
    tn, bk, d = x_ref.shape
    npv = CONV_W - 1
    mod = _mod_getter(m_ref, True)
    x = x_ref[...]
    h = _rms_mod(x, g_ref[...][None], mod(0), mod(1))
    u = _glu_rows(h.reshape(tn * bk, d).astype(BF16), w1_ref, b1_ref, d).reshape(tn, bk, d)
    s_ref[0:npv] = p_ref[...]
    s_ref[npv:npv + tn] = u
    acc = jnp.broadcast_to(bdw_ref[...][None], (tn, bk, d))
    for w in range(CONV_W):
        acc = acc + wdw_ref[w:w + 1, :][None] * s_ref[w:w + tn]
    z = _conv_core(acc, lng_ref[...][None], lnb_ref[...][None]).astype(BF16)
    y2 = _dot(z.reshape(tn * bk, d), w2_ref[...]).reshape(tn, bk, d)
    xo_ref[...] = x + mod(2) * y2
    st_ref[...] = s_ref[tn:tn + npv]


def _conv_layer(x, prev_tm, mods, mods_rows, l, time_major, g1, w1, b1, wdw, bdw, lng, lnb, w2):
    tl = _Tiling(x.shape, time_major, mods_rows)
    d = tl.d
    weights = [g1.reshape(1, d), w1, b1.reshape(1, 2 * d), wdw, bdw.reshape(1, d), lng.reshape(1, d),
               lnb.reshape(1, d), w2]
    w_specs = [_const_spec(w.shape) for w in weights]
    if time_major:
        tn, bd, _ = x.shape
        bk = tl.tile[1]
        npv = CONV_W - 1
        return pl.pallas_call(
            _conv_sample_kernel,
            grid=tl.grid,
            in_specs=[tl.x_spec(), pl.BlockSpec((npv, bk, d), lambda j: (0, j, 0)), tl.mod_spec(6, l)] + w_specs,
            out_specs=[tl.x_spec(), pl.BlockSpec((npv, bk, d), lambda j: (0, j, 0))],
            out_shape=[jax.ShapeDtypeStruct(x.shape, F32), jax.ShapeDtypeStruct((npv, bd, d), F32)],
            scratch_shapes=[pltpu.VMEM((npv + tn, bk, d), F32)],
            compiler_params=_cparams(tl.sem),
            name="conv_sample",
        )(x, prev_tm, mods, *weights)
    b, t, _ = x.shape
    tr = tl.rows
    return pl.pallas_call(
        _conv_prompt_kernel,
        grid=tl.grid,
        in_specs=[tl.x_spec(), tl.mod_spec(6, l)] + w_specs,
        out_specs=[tl.x_spec(), pl.BlockSpec((1, 32, d), lambda bi, i: (bi, 0, 0))],
        out_shape=[jax.ShapeDtypeStruct(x.shape, F32), jax.ShapeDtypeStruct((b, 32, d), F32)],
        scratch_shapes=[pltpu.VMEM((tr + 32 + SUBLANES, d), F32), pltpu.VMEM((tr + SUBLANES, d), F32)],
        compiler_params=_cparams(tl.sem),
        name="conv_prompt",
    )(x, mods, *weights)


def _ffn_kernel(*refs, time_major, has_attn, final, head_major_o):
    it = iter(refs)
    x_ref = next(it)
    m_ref = next(it)
    g2_ref = next(it)
    if has_attn:
        o_ref = next(it)
        wo_ref = next(it)
    wup_ref = next(it)
    wdn_ref = next(it)
    if final:
        fm_ref = next(it)
        fg_ref = next(it)
    out_ref = next(it)
    hb_ref = next(it)
    acc_ref = next(it)

    a, bk, d = x_ref.shape
    rows = a * bk
    mod = _mod_getter(m_ref, time_major)
    x = x_ref[...]
    if has_attn:
        if head_major_o:
            ocat = jnp.concatenate([o_ref[0, h] for h in range(N_HEADS)], axis=1)
        else:
            ocat = o_ref[...].reshape(rows, o_ref.shape[-1]).astype(BF16)
        x = x + mod(2) * _dot(ocat, wo_ref[...]).reshape(a, bk, d)
    h = _rms_mod(x, g2_ref[...][None], mod(3), mod(4))
    hb_ref[...] = h.reshape(rows, d).astype(BF16)
    acc_ref[...] = jnp.zeros((rows, d), F32)
    c = wdn_ref.shape[1]

    def body(j, carry):
        t = _dot(hb_ref[...], wup_ref[j])
        act = (_silu(t[:, :c]) * t[:, c:]).astype(BF16)
        acc_ref[...] += _dot(act, wdn_ref[j])
        return carry

    lax.fori_loop(0, wup_ref.shape[0], body, 0)
    x = x + mod(5) * acc_ref[...].reshape(a, bk, d)
    if final:
        fmod = _mod_getter(fm_ref, time_major)
        x = _rms_mod(x, fg_ref[...][None], fmod(0), fmod(1))
    out_ref[...] = x


def _ffn_layer(x, mods, mods_rows, l, time_major, g2, wup_c, wdn_c, o=None, wo=None, final_g=None, n_mod_final=None):
    tl = _Tiling(x.shape, time_major, mods_rows)
    d = tl.d
    has_attn = o is not None
    final = final_g is not None
    args = [x, mods, g2.reshape(1, d)]
    specs = [tl.x_spec(), tl.mod_spec(6, l), _const_spec((1, d))]
    head_major_o = False
    if has_attn:
        if time_major:
            specs.append(pl.BlockSpec(tl.tile[:2] + (o.shape[-1],), tl.x_map))
        else:
            head_major_o = True
            tr = tl.rows
            specs.append(pl.BlockSpec((1, N_HEADS, tr, LANES), lambda bi, i: (bi, 0, i, 0)))
        args += [o, wo]
        specs.append(_const_spec(wo.shape))
    args += [wup_c, wdn_c]
    specs += [_const_spec(wup_c.shape), _const_spec(wdn_c.shape)]
    if final:
        args += [mods, final_g.reshape(1, d)]
        specs += [tl.mod_spec(2, n_mod_final), _const_spec((1, d))]
    kern = functools.partial(_ffn_kernel, time_major=time_major, has_attn=has_attn, final=final,
                             head_major_o=head_major_o)
    return pl.pallas_call(
        kern,
        grid=tl.grid,
        in_specs=specs,
        out_specs=tl.x_spec(),
        out_shape=jax.ShapeDtypeStruct(x.shape, F32),
        scratch_shapes=[pltpu.VMEM((tl.rows, d), BF16), pltpu.VMEM((tl.rows, d), F32)],
        compiler_params=_cparams(tl.sem),
        name="ffn_sample" if time_major else "ffn_prompt",
    )(*args)


def _kv_prompt_kernel(x_ref, m_ref, g_ref, wt_ref, wp_ref, kvt_ref, wint_ref, ks_ref, kw_ref, vs_ref, vw_ref):
    tr, d = x_ref.shape[1], x_ref.shape[2]
    mod = _mod_getter(m_ref, False)
    hk = _rms_mod(x_ref[...], g_ref[...][None], mod(0), mod(1)).reshape(tr, d).astype(BF16)
    kvt = _dot_nt(wt_ref[...], hk)
    n4 = kvt_ref.shape[1]
    for p in range(tr // PAGE):
        kvt_ref[p] = kvt[:n4, p * PAGE:(p + 1) * PAGE]
    wint_ref[0] = kvt[n4:, :]
    att = _dot(hk, wp_ref[...])
    t0 = pl.program_id(1) * tr
    blk = (t0 + lax.broadcasted_iota(jnp.int32, (tr, LANES), 0)) // L_SEL
    lane = lax.broadcasted_iota(jnp.int32, (tr, LANES), 1)
    onehot = lane - HEAD_DIM == blk
    low = lane < HEAD_DIM
    for k in range(N_KV):
        def part(j):
            return att[:, (j * N_KV + k) * LANES:(j * N_KV + k + 1) * LANES]
        ks_ref[0, k] = jnp.where(onehot, 1.0, part(0)).astype(BF16)
        kw_ref[0, k] = part(1).astype(BF16)
        vs_ref[0, k] = jnp.where(low, part(2), 1.0).astype(BF16)
        vw_ref[0, k] = jnp.where(low, 1.0, part(3)).astype(BF16)


def _kv_sample_kernel(x_ref, m_ref, g_ref, wt_ref, kvt_ref):
    tn, bk, d = x_ref.shape
    mod = _mod_getter(m_ref, True)
    hk = _rms_mod(x_ref[...], g_ref[...][None], mod(0), mod(1)).reshape(tn * bk, d).astype(BF16)
    kvt_ref[...] = _dot_nt(wt_ref[...], hk)


def _kv_proj(x, mods, mods_rows, kblk, time_major, g, w_t, w_perm):
    d = x.shape[-1]
    nkv = w_t.shape[0]
    if time_major:
        tn, bd, _ = x.shape
        tl = _Tiling(x.shape, True, mods_rows, bk=bd)
        return pl.pallas_call(
            _kv_sample_kernel,
            grid=tl.grid,
            in_specs=[tl.x_spec(), tl.mod_spec(2, kblk), _const_spec((1, d)), _const_spec(w_t.shape)],
            out_specs=pl.BlockSpec((nkv, tn * bd), lambda j: (0, 0)),
            out_shape=jax.ShapeDtypeStruct((nkv, tn * bd), F32),
            compiler_params=_cparams(tl.sem),
            name="kv_sample",
        )(x, mods, g.reshape(1, d), w_t)
    tl = _Tiling(x.shape, False, mods_rows)
    b, t, _ = x.shape
    tr = tl.rows
    n4 = 4 * N_KV * HEAD_DIM
    nw = min(WINDOW, t)
    assert tr == nw and tr % PAGE == 0
    ppt = tr // PAGE
    hm = pl.BlockSpec((1, N_KV, tr, LANES), lambda bi, i: (bi, 0, i, 0))
    hm_shape = jax.ShapeDtypeStruct((b, N_KV, t, LANES), BF16)
    return pl.pallas_call(
        _kv_prompt_kernel,
        grid=tl.grid,
        in_specs=[tl.x_spec(), tl.mod_spec(2, kblk), _const_spec((1, d)), _const_spec(w_t.shape),
                  _const_spec(w_perm.shape)],
        out_specs=[pl.BlockSpec((ppt, n4, PAGE), lambda bi, i: (bi * (t // tr) + i, 0, 0)),
                   pl.BlockSpec((1, nkv - n4, nw), lambda bi, i: (bi, 0, 0)), hm, hm, hm, hm],
        out_shape=[jax.ShapeDtypeStruct((b * t // PAGE, n4, PAGE), F32),
                   jax.ShapeDtypeStruct((b, nkv - n4, nw), F32), hm_shape, hm_shape, hm_shape, hm_shape],
        compiler_params=_cparams(tl.sem),
        name="kv_prompt",
    )(x, mods, g.reshape(1, d), w_t, w_perm)


def _compress_kernel(pt_ref, *refs, spb):
    del pt_ref
    n_pages = spb * PAGES_PER_STEP
    pages = refs[:n_pages]
    w1_ref, w1f_ref, pe_ref, w2p_ref, w2n_ref = refs[n_pages:n_pages + 5]
    kcp_ref, vcp_ref, kcn_ref, vcn_ref = refs[n_pages + 5:n_pages + 9]
    x_ref, hs_ref = refs[n_pages + 9:]
    nh = x_ref.shape[1] // spb
    rows_all = spb * nh
    hid = w2p_ref.shape[1]
    per_page = PAGE // STRIDE
    pg0 = pl.program_id(1) * PAGES_PER_STEP

    pi = lax.broadcasted_iota(jnp.int32, (PAGE, PAGE), 0)
    pj = lax.broadcasted_iota(jnp.int32, (PAGE, PAGE), 1)
    perm = jnp.where(pj == (pi % per_page) * STRIDE + pi // per_page, 1.0, 0.0).astype(BF16)
    for s in range(spb):
        for pg in range(PAGES_PER_STEP):
            r0 = pl.multiple_of(s * nh + (pg0 + pg) * per_page, per_page)
            rp = _dot_nt(perm, pages[s * PAGES_PER_STEP + pg][0].astype(BF16))
            for l in range(STRIDE):
                x_ref[l, pl.ds(r0, per_page), :] = rp[l * per_page:(l + 1) * per_page, :]

    @pl.when(pl.program_id(1) == pl.num_programs(1) - 1)
    def _():
        rows = lax.broadcasted_iota(jnp.int32, (rows_all, hid), 0) % nh
        hs_ref[rows_all:rows_all + 8, :] = jnp.zeros((8, hid), F32)
        for c, (pad_ref, nat_ref) in enumerate(((kcp_ref, kcn_ref), (vcp_ref, vcn_ref))):
            cvec = _dot(pe_ref[c], w1f_ref[c])[0:1, :]
            nat = jnp.zeros((rows_all, N_KV * HEAD_DIM), F32)
            for pr in range(N_KV // 2):
                lo = c * N_KV * HEAD_DIM + pr * LANES
                xcat = jnp.concatenate([x_ref[l, :, lo:lo + LANES] for l in range(STRIDE)], axis=1)
                acc = _dot(xcat.astype(BF16), w1_ref[c])
                for e in range(2):
                    k = 2 * pr + e
                    hs_ref[0:rows_all, :] = acc[:, (2 * e + 1) * hid:(2 * e + 2) * hid]
                    hsum = acc[:, 2 * e * hid:(2 * e + 1) * hid] + hs_ref[pl.ds(1, rows_all), :] + cvec
                    hb = jnp.where(rows < nh - 1, _silu(hsum), 0.0).astype(BF16)
                    padded = _dot(hb, w2p_ref[c]).astype(BF16)
                    for s in range(spb):
                        pad_ref[s, k] = padded[s * nh:(s + 1) * nh]
                    nat = nat + _dot(hb, w2n_ref[c, k])
            for s in range(spb):
                nat_ref[s] = nat[s * nh:(s + 1) * nh].astype(BF16)


def _compress(pages, page_table, cw):
    bs, p = page_table.shape
    assert p % PAGES_PER_STEP == 0
    ng = p // PAGES_PER_STEP
    nh = p * (PAGE // STRIDE)
    wcols = 2 * N_KV * HEAD_DIM
    hid = cw["w2p"].shape[1]
    spb = 2 if bs % 2 == 0 else 1

    def page_spec(s, pg):
        return pl.BlockSpec((1, wcols, PAGE), lambda b, g, pt: (pt[b * spb + s, g * PAGES_PER_STEP + pg], 0, 0))

    def full(a):
        nd = a.ndim
        return pl.BlockSpec(a.shape, lambda b, g, pt: (0,) * nd)

    consts = [cw["w1"], cw["w1f"], cw["pe"], cw["w2p"], cw["w2n"]]
    pad_shape = jax.ShapeDtypeStruct((bs, N_KV, nh, LANES), BF16)
    nat_shape = jax.ShapeDtypeStruct((bs, nh, N_KV * HEAD_DIM), BF16)
    pad_spec = pl.BlockSpec((spb, N_KV, nh, LANES), lambda b, g, pt: (b, 0, 0, 0))
    nat_spec = pl.BlockSpec((spb, nh, N_KV * HEAD_DIM), lambda b, g, pt: (b, 0, 0))
    grid_spec = pltpu.PrefetchScalarGridSpec(
        num_scalar_prefetch=1,
        grid=(bs // spb, ng),
        in_specs=[page_spec(s, pg) for s in range(spb) for pg in range(PAGES_PER_STEP)] + [full(a) for a in consts],
        out_specs=[pad_spec, pad_spec, nat_spec, nat_spec],
        scratch_shapes=[pltpu.VMEM((STRIDE, spb * nh, wcols), F32), pltpu.VMEM((spb * nh + 8, hid), F32)],
    )
    return pl.pallas_call(
        functools.partial(_compress_kernel, spb=spb),
        grid_spec=grid_spec,
        out_shape=[pad_shape, pad_shape, nat_shape, nat_shape],
        compiler_params=_cparams(("arbitrary", "arbitrary")),
        name="compress",
    )(page_table, *([pages] * (spb * PAGES_PER_STEP)), *consts)


def _qin_prompt_kernel(x_ref, m_ref, g_ref, w_ref, q_ref, gate_ref):
    tr, d = x_ref.shape[1], x_ref.shape[2]
    mod = _mod_getter(m_ref, False)
    h = _rms_mod(x_ref[...], g_ref[...][None], mod(0), mod(1)).reshape(tr, d).astype(BF16)
    a = _dot(h, w_ref[...])
    for hd in range(N_HEADS):
        q_ref[0, hd] = a[:, hd * LANES:(hd + 1) * LANES].astype(BF16)
    gate_ref[0] = a[:, N_HEADS * LANES:]


def _qin_sample_kernel(x_ref, m_ref, g_ref, w_ref, a_ref):
    tn, bk, d = x_ref.shape
    mod = _mod_getter(m_ref, True)
    h = _rms_mod(x_ref[...], g_ref[...][None], mod(0), mod(1)).reshape(tn * bk, d).astype(BF16)
    a_ref[...] = _dot(h, w_ref[...]).reshape(tn, bk, a_ref.shape[-1])


def _qin(x, mods, mods_rows, l, time_major, g1, w):
    tl = _Tiling(x.shape, time_major, mods_rows)
    d = tl.d
    n = w.shape[1]
    common = dict(grid=tl.grid, compiler_params=_cparams(tl.sem))
    in_specs = [tl.x_spec(), tl.mod_spec(6, l), _const_spec((1, d)), _const_spec(w.shape)]
    if time_major:
        return pl.pallas_call(
            _qin_sample_kernel, in_specs=in_specs,
            out_specs=pl.BlockSpec(tl.tile[:2] + (n,), tl.x_map),
            out_shape=jax.ShapeDtypeStruct(x.shape[:2] + (n,), F32),
            name="qin_sample", **common)(x, mods, g1.reshape(1, d), w)
    b, t, _ = x.shape
    tr = tl.rows
    return pl.pallas_call(
        _qin_prompt_kernel, in_specs=in_specs,
        out_specs=[pl.BlockSpec((1, N_HEADS, tr, LANES), lambda bi, i: (bi, 0, i, 0)),
                   pl.BlockSpec((1, tr, LANES), tl.x_map)],
        out_shape=[jax.ShapeDtypeStruct((b, N_HEADS, t, LANES), BF16), jax.ShapeDtypeStruct((b, t, LANES), F32)],
        name="qin_prompt", **common)(x, mods, g1.reshape(1, d), w)


def _block_rank(score_ref, ns):
    score = score_ref[...]
    blk = lax.broadcasted_iota(jnp.int32, score.shape, 0)
    rank = jnp.zeros(score.shape, F32)
    for i in range(ns):
        si = score_ref[i:i + 1, :]
        rank = rank + jnp.where(blk > i, jnp.where(si >= score, 1.0, 0.0), jnp.where(si > score, 1.0, 0.0))
    return rank


def _add_alibi(s, slopes_ref, head0, pos_f, qb):
    return jnp.concatenate([s[g * qb:(g + 1) * qb] + slopes_ref[head0 + g] * pos_f for g in range(GROUP)], axis=0)


def _attn_prompt_kernel(slopes_ref, q_ref, gate_ref, ks_ref, kw_ref, vs_ref, vw_ref, kc_ref, vc_ref, msel_ref,
                        ega_ref, egb_ref, o_ref,
                        s_ref, sc_ref, mt_ref, qa_ref, oc_ref, as_ref, mx_ref, ga_ref, gb_ref, ids_ref, cnt_ref):
    qb = q_ref.shape[2]
    rows = GROUP * qb
    t = ks_ref.shape[2]
    nc = kc_ref.shape[2]
    ns = msel_ref.shape[0]
    max_tiles = t // SEL_TILE
    blocks_per_tile = SEL_TILE // L_SEL
    i = pl.program_id(1)
    s0 = i * qb
    n_tiles = (s0 + qb + SEL_TILE - 1) // SEL_TILE
    qpos4 = s0 + lax.broadcasted_iota(jnp.int32, (rows, 1), 0) % qb
    low4 = lax.broadcasted_iota(jnp.int32, (rows, LANES), 1) < HEAD_DIM

    gs = jax.nn.sigmoid(gate_ref[0])
    g_hi = gs.astype(BF16)
    g_lo = (gs - g_hi.astype(F32)).astype(BF16)
    g2 = jnp.concatenate([g_hi, g_lo], axis=1)
    ga = _dot(g2, ega_ref[...])
    gb = _dot(g2, egb_ref[...])
    for h in range(N_HEADS):
        ga_ref[h] = ga[:, h * LANES:(h + 1) * LANES]
        gb_ref[h] = gb[:, h * LANES:(h + 1) * LANES]

    cidx = lax.broadcasted_iota(jnp.int32, (1, nc), 1)
    cend = cidx * STRIDE + (L_CMP - 1)
    cmask = (cend <= qpos4) & (cidx < nc - 1)
    cend_f = cend.astype(F32)
    row_ok = qpos4 >= L_CMP - 1
    blk = lax.broadcasted_iota(jnp.int32, (ns, qb), 0)
    qpos_t = s0 + lax.broadcasted_iota(jnp.int32, (ns, qb), 1)
    cur = qpos_t // L_SEL
    valid_t = blk * L_SEL <= qpos_t
    forced = (blk == 0) | (blk == cur) | (blk == cur - 1)
    tile_of_blk = jnp.where(lax.broadcasted_iota(jnp.int32, (max_tiles, ns), 1) // blocks_per_tile
                            == lax.broadcasted_iota(jnp.int32, (max_tiles, ns), 0), 1.0, 0.0).astype(BF16)

    for kvh in range(N_KV):
        q4 = q_ref[0, kvh * GROUP:(kvh + 1) * GROUP].reshape(rows, LANES)
        sc = _add_alibi(_dot_nt(q4, kc_ref[0, kvh]), slopes_ref, kvh * GROUP, cend_f, qb)
        sc = jnp.where(cmask, sc, NEG)
        e = jnp.exp(sc - jnp.max(sc, axis=1, keepdims=True))
        r = jnp.where(row_ok, 1.0 / jnp.sum(e, axis=1, keepdims=True), 0.0)
        pb = (e * r).astype(BF16)
        oc_ref[kvh] = _dot(pb, vc_ref[0, kvh])
        imp4 = _dot_nt(msel_ref[...], pb)
        imp_t = imp4[:, 0:qb]
        for g in range(1, GROUP):
            imp_t = imp_t + imp4[:, g * qb:(g + 1) * qb]
        sc_ref[kvh] = jnp.where(valid_t, imp_t + jnp.where(forced, FORCE, 0.0), -jnp.inf)
        keep = (_block_rank(sc_ref.at[kvh], ns) < TOP_N) & valid_t
        mt_ref[kvh] = jnp.zeros(mt_ref.shape[1:], F32)
        mt_ref[kvh, HEAD_DIM:HEAD_DIM + ns, :] = jnp.where(keep, 0.0, NEG)
        mask_b = mt_ref[kvh].T.astype(BF16)
        qa_ref[kvh] = jnp.where(low4, q4, jnp.concatenate([mask_b] * GROUP, axis=0))

        per_tile = jnp.sum(_dot(tile_of_blk, jnp.where(keep, 1.0, 0.0).astype(BF16)), axis=1, keepdims=True)
        for pos in range(ids_ref.shape[1]):
            ids_ref[kvh, pos] = jnp.int32(-1)
        n_act = jnp.int32(0)
        for kt in range(max_tiles - 1):
            active = jnp.logical_and(kt < n_tiles - 1, per_tile[kt, 0] > 0.5)
            ids_ref[kvh, n_act] = jnp.where(active, jnp.int32(kt), jnp.int32(-1))
            n_act = n_act + active.astype(jnp.int32)
        ids_ref[kvh, n_act] = n_tiles - 1
        cnt_ref[kvh] = n_act + 1

    lane_blocks = SEL_TILE // LANES
    for pair in range(N_KV // 2):
        kvs = (2 * pair, 2 * pair + 1)
        n_steps = (jnp.maximum(cnt_ref[kvs[0]], cnt_ref[kvs[1]]) + 1) // 2
        for kvh in kvs:
            mx_ref[kvh] = jnp.full((rows, LANES), NEG, F32)
            as_ref[kvh] = jnp.zeros((rows, LANES), F32)

        def entries(step):
            out = []
            for u in range(2):
                for j, kvh in enumerate(kvs):
                    kt = ids_ref[kvh, 2 * step + u]
                    null = kt < 0
                    out.append((j, kvh, jnp.maximum(kt, 0), jnp.where(null, max_tiles, kt), null))
            return out

        def score_step(step, carry):
            for j, kvh, kt, slot, null in entries(step):
                k0 = pl.multiple_of(kt * SEL_TILE, SEL_TILE)
                kpos = k0 + lax.broadcasted_iota(jnp.int32, (1, SEL_TILE), 1)
                s = _dot_nt(qa_ref[kvh], ks_ref[0, kvh, pl.ds(k0, SEL_TILE), :])
                s = _add_alibi(s, slopes_ref, kvh * GROUP, kpos.astype(F32), qb)
                s = jnp.where(kpos + jnp.where(null, jnp.int32(1 << 30), 0) <= qpos4, s, NEG)
                s_ref[j, slot] = s
                m = s[:, 0:LANES]
                for c in range(1, lane_blocks):
                    m = jnp.maximum(m, s[:, c * LANES:(c + 1) * LANES])
                mx_ref[kvh] = jnp.maximum(mx_ref[kvh], m)
            return carry

        lax.fori_loop(0, n_steps, score_step, 0)
        mrow = [jnp.max(mx_ref[kvh], axis=1, keepdims=True) for kvh in kvs]

        def weight_step(step, carry):
            for j, kvh, kt, slot, null in entries(step):
                k0 = pl.multiple_of(kt * SEL_TILE, SEL_TILE)
                p = jnp.exp(s_ref[j, slot] - mrow[j]).astype(BF16)
                as_ref[kvh] += _dot(p, vs_ref[0, kvh, pl.ds(k0, SEL_TILE), :])
            return carry

        lax.fori_loop(0, n_steps, weight_step, 0)

    wlen = WINDOW + qb
    w0 = pl.multiple_of(jnp.maximum(s0 - WINDOW, 0), qb)
    wpos = w0 + lax.broadcasted_iota(jnp.int32, (1, wlen), 1)
    wpos_f = wpos.astype(F32)
    dist = qpos4 - wpos
    wmask = (dist >= 0) & (dist < WINDOW)
    for kvh in range(N_KV):
        q4 = q_ref[0, kvh * GROUP:(kvh + 1) * GROUP].reshape(rows, LANES)
        s = _add_alibi(_dot_nt(q4, kw_ref[0, kvh, pl.ds(w0, wlen), :]), slopes_ref, kvh * GROUP, wpos_f, qb)
        s = jnp.where(wmask, s, NEG)
        p = jnp.exp(s - jnp.max(s, axis=1, keepdims=True)).astype(BF16)
        acc_w = _dot(p, vw_ref[0, kvh, pl.ds(w0, wlen), :])
        acc_s = as_ref[kvh]
        o_s = acc_s * (1.0 / pltpu.roll(acc_s, HEAD_DIM, 1))
        o_w = acc_w * (1.0 / pltpu.roll(acc_w, HEAD_DIM, 1))
        ga4 = ga_ref[kvh * GROUP:(kvh + 1) * GROUP].reshape(rows, LANES)
        gb4 = gb_ref[kvh * GROUP:(kvh + 1) * GROUP].reshape(rows, LANES)
        out = ga4 * jnp.where(low4, oc_ref[kvh], o_w) + gb4 * jnp.where(low4, o_s, 0.0)
        o_ref[0, kvh * GROUP:(kvh + 1) * GROUP] = out.reshape(GROUP, qb, LANES).astype(BF16)


def _gate_expand():
    src = jnp.arange(2 * LANES) % LANES
    br = (src // N_HEADS)[:, None]
    hd = (src % N_HEADS)[:, None]
    col = jnp.arange(N_HEADS * LANES)
    same = hd == (col // LANES)[None, :]
    low = (col % LANES < HEAD_DIM)[None, :]
    ea = same & (((br == 0) & low) | ((br == 2) & ~low))
    eb = same & (br == 1) & low
    return ea.astype(BF16), eb.astype(BF16)


def _attn_prompt(slopes, q, gates, ks, kw, vs, vw, kcp, vcp, msel_t):
    b, _, t, _ = q.shape
    nc = kcp.shape[2]
    ns = msel_t.shape[0]
    assert t % SEL_TILE == 0 and t >= WINDOW + Q_BLK and ns <= HEAD_DIM and ns % 8 == 0
    ea, eb = _gate_expand()
    rows = GROUP * Q_BLK
    max_tiles = t // SEL_TILE
    qspec = pl.BlockSpec((1, N_HEADS, Q_BLK, LANES), lambda bi, i: (bi, 0, i, 0))
    kvspec = pl.BlockSpec((1, N_KV, t, LANES), lambda bi, i: (bi, 0, 0, 0), pipeline_mode=pl.Buffered(1))
    cspec = pl.BlockSpec((1, N_KV, nc, LANES), lambda bi, i: (bi, 0, 0, 0))
    return pl.pallas_call(
        _attn_prompt_kernel,
        grid=(b, t // Q_BLK),
        in_specs=[pl.BlockSpec(memory_space=pltpu.SMEM), qspec,
                  pl.BlockSpec((1, Q_BLK, LANES), lambda bi, i: (bi, i, 0)),
                  kvspec, kvspec, kvspec, kvspec, cspec, cspec,
                  _const_spec(msel_t.shape), _const_spec(ea.shape), _const_spec(eb.shape)],
        out_specs=qspec,
        out_shape=jax.ShapeDtypeStruct(q.shape, BF16),
        scratch_shapes=[pltpu.VMEM((2, max_tiles + 1, rows, SEL_TILE), F32),
                        pltpu.VMEM((N_KV, ns, Q_BLK), F32), pltpu.VMEM((N_KV, LANES, Q_BLK), F32),
                        pltpu.VMEM((N_KV, rows, LANES), BF16), pltpu.VMEM((N_KV, rows, LANES), F32),
                        pltpu.VMEM((N_KV, rows, LANES), F32), pltpu.VMEM((N_KV, rows, LANES), F32),
                        pltpu.VMEM((N_HEADS, Q_BLK, LANES), F32), pltpu.VMEM((N_HEADS, Q_BLK, LANES), F32),
                        pltpu.SMEM((N_KV, max_tiles + 2), jnp.int32), pltpu.SMEM((N_KV,), jnp.int32)],
        compiler_params=_cparams(("arbitrary", "arbitrary")),
        name="attn_prompt",
    )(slopes, q, gates, ks, kw, vs, vw, kcp, vcp, msel_t, ea, eb)


def _attn_sample_kernel(pt_ref, *refs, past):
    del pt_ref
    npg = past // PAGE
    pages = refs[:npg]
    (q_ref, gate_ref, kc_ref, vc_ref, sw_ref, kvn_ref, slope_ref, msel_ref, o_ref,
     s_ref, sc_ref, st_ref) = refs[npg:]
    tn = q_ref.shape[1]
    hw = N_KV * HEAD_DIM
    rows = N_KV * GROUP * tn
    nc = kc_ref.shape[1]
    ns_pad = sc_ref.shape[0]
    nblk = npg + 1
    n_s = (past + tn + L_SEL - 1) // L_SEL
    f_sel = 2 * hw
    slope = slope_ref[...]

    rr = lax.broadcasted_iota(jnp.int32, (rows, 1), 0)
    qpos = past + rr % tn
    col = lax.broadcasted_iota(jnp.int32, (tn, hw), 1) // HEAD_DIM
    colr = lax.broadcasted_iota(jnp.int32, (rows, hw), 1) // HEAD_DIM
    rowk = lax.broadcasted_iota(jnp.int32, (rows, hw), 0) // (GROUP * tn)
    diag = colr == rowk

    qs = q_ref[0]
    pieces = []
    for k in range(N_KV):
        for g in range(GROUP):
            pieces.append(jnp.where(col == k, qs[:, g * hw:(g + 1) * hw], 0.0))
    qbd = jnp.concatenate(pieces, axis=0).astype(BF16)

    cidx = lax.broadcasted_iota(jnp.int32, (1, nc), 1)
    cend = cidx * STRIDE + (L_CMP - 1)
    cvalid = (cend <= qpos) & (cidx < nc - 1)
    sc = jnp.where(cvalid, _dot_nt(qbd, kc_ref[0]) + slope * cend.astype(F32), NEG)
    e = jnp.exp(sc - jnp.max(sc, axis=1, keepdims=True))
    r = jnp.where(qpos >= L_CMP - 1, 1.0 / jnp.sum(e, axis=1, keepdims=True), 0.0)
    pc = (e * r).astype(BF16)
    o_c = _dot(pc, vc_ref[0])

    imp = _dot_nt(msel_ref[...], pc)
    tot = imp
    for g in range(1, GROUP):
        tot = tot + pltpu.roll(imp, rows - g * tn, 1)
    blk = lax.broadcasted_iota(jnp.int32, (ns_pad, rows), 0)
    lane_r = lax.broadcasted_iota(jnp.int32, (ns_pad, rows), 1)
    qpos_l = past + lane_r % tn
    cur = qpos_l // L_SEL
    valid_l = (blk * L_SEL <= qpos_l) & (blk < n_s)
    forced = (blk == 0) | (blk == cur) | (blk == cur - 1)
    sc_ref[...] = jnp.where(valid_l, tot + jnp.where(forced, FORCE, 0.0), -jnp.inf)
    keep = (_block_rank(sc_ref, n_s) < TOP_N) & valid_l
    lead = (lane_r // tn) % GROUP == 0
    keep_f = jnp.where(keep & lead, 1.0, 0.0)
    spread = keep_f
    for g in range(1, GROUP):
        spread = spread + pltpu.roll(keep_f, g * tn, 1)
    st_ref[...] = jnp.zeros(st_ref.shape, F32)
    st_ref[0:ns_pad, :] = spread
    keep_rows = st_ref[...].T.astype(BF16)

    zpad = jnp.zeros((PAGE - tn, hw), F32)
    bsel = lax.broadcasted_iota(jnp.int32, (LANES, PAGE), 0)
    ksub = lax.broadcasted_iota(jnp.int32, (LANES, PAGE), 1) // L_SEL
    kloc = lax.broadcasted_iota(jnp.int32, (1, PAGE), 1)
    mrun = jnp.full((rows, 1), NEG, F32)
    for j in range(nblk):
        if j < npg:
            qk = _dot(qbd, pages[j][0, 0:hw, :].astype(BF16))
        else:
            kb = jnp.concatenate([kvn_ref[0, :, f_sel:f_sel + hw], zpad], axis=0).astype(BF16)
            qk = _dot_nt(qbd, kb)
        kpos = j * PAGE + kloc
        expand = jnp.where(bsel == (PAGE // L_SEL) * j + ksub, 1.0, 0.0).astype(BF16)
        kept = _dot(keep_rows, expand)
        s = jnp.where((kept > 0.5) & (kpos <= qpos), qk + slope * kpos.astype(F32), NEG)
        s_ref[:, j * PAGE:(j + 1) * PAGE] = s
        mrun = jnp.maximum(mrun, jnp.max(s, axis=1, keepdims=True))
    lsum = jnp.zeros((rows, 1), F32)
    o_s = jnp.zeros((rows, hw), F32)
    for j in range(nblk):
        p = jnp.exp(s_ref[:, j * PAGE:(j + 1) * PAGE] - mrun)
        lsum = lsum + jnp.sum(p, axis=1, keepdims=True)
        if j < npg:
            o_s = o_s + _dot_nt(p.astype(BF16), pages[j][0, hw:2 * hw, :].astype(BF16))
        else:
            vb = jnp.concatenate([kvn_ref[0, :, f_sel + hw:f_sel + 2 * hw], zpad], axis=0).astype(BF16)
            o_s = o_s + _dot(p.astype(BF16), vb)
    o_s = o_s * (1.0 / lsum)

    nbuf = sw_ref.shape[2]
    f_win = 4 * hw
    kwn = jnp.concatenate([kvn_ref[0, :, f_win:f_win + hw], zpad], axis=0).astype(BF16)
    vwn = jnp.concatenate([kvn_ref[0, :, f_win + hw:f_win + 2 * hw], zpad], axis=0).astype(BF16)
    pos_a = past - nbuf + lax.broadcasted_iota(jnp.int32, (1, nbuf), 1)
    pos_b = past + kloc
    da = qpos - pos_a
    db = qpos - pos_b
    va = (da >= 0) & (da < WINDOW) & (pos_a >= 0)
    vb_ok = (db >= 0) & (db < WINDOW)
    s_a = jnp.where(va, _dot(qbd, sw_ref[0, 0:hw, :].astype(BF16)) + slope * pos_a.astype(F32), NEG)
    s_b = jnp.where(vb_ok, _dot_nt(qbd, kwn) + slope * pos_b.astype(F32), NEG)
    m = jnp.maximum(jnp.max(s_a, axis=1, keepdims=True), jnp.max(s_b, axis=1, keepdims=True))
    p_a = jnp.exp(s_a - m)
    p_b = jnp.exp(s_b - m)
    l_w = jnp.sum(p_a, axis=1, keepdims=True) + jnp.sum(p_b, axis=1, keepdims=True)
    o_w = (_dot_nt(p_a.astype(BF16), sw_ref[0, hw:2 * hw, :].astype(BF16)) + _dot(p_b.astype(BF16), vwn)) * (1.0 / l_w)

    gt = jax.nn.sigmoid(gate_ref[0])
    tot_o = jnp.where(diag, gt[0] * o_c + gt[1] * o_s + gt[2] * o_w, 0.0)
    for g in range(GROUP):
        acc = jnp.zeros((tn, hw), F32)
        for k in range(N_KV):
            r0 = (k * GROUP + g) * tn
            acc = acc + tot_o[r0:r0 + tn, :]
        o_ref[0, :, g * hw:(g + 1) * hw] = acc


def _attn_sample(page_table, cache_t, q, gates, kcn, vcn, state_win_t, kv_new, slope_rows, msel_t, past):
    bd, tn, qw = q.shape
    npg = past // PAGE
    hw = N_KV * HEAD_DIM
    rows = N_KV * GROUP * tn
    assert rows == LANES and msel_t.shape[1] == kcn.shape[1]
    ns_pad = msel_t.shape[0]

    def page_spec(pg):
        return pl.BlockSpec((1, 2 * hw, PAGE), lambda b, pt: (pt[b, pg], 1, 0))

    def per_seq(a):
        nd = a.ndim
        return pl.BlockSpec((1,) + a.shape[1:], lambda b, pt: (b,) + (0,) * (nd - 1))

    def full(a):
        nd = a.ndim
        return pl.BlockSpec(a.shape, lambda b, pt: (0,) * nd)

    grid_spec = pltpu.PrefetchScalarGridSpec(
        num_scalar_prefetch=1,
        grid=(bd,),
        in_specs=[page_spec(pg) for pg in range(npg)]
        + [per_seq(q), per_seq(gates), per_seq(kcn), per_seq(vcn), per_seq(state_win_t), per_seq(kv_new),
           full(slope_rows), full(msel_t)],
        out_specs=pl.BlockSpec((1, tn, N_HEADS * HEAD_DIM), lambda b, pt: (b, 0, 0)),
        scratch_shapes=[pltpu.VMEM((rows, (npg + 1) * PAGE), F32), pltpu.VMEM((ns_pad, rows), F32),
                        pltpu.VMEM((LANES, rows), F32)],
    )
    return pl.pallas_call(
        functools.partial(_attn_sample_kernel, past=past),
        grid_spec=grid_spec,
        out_shape=jax.ShapeDtypeStruct((bd, tn, N_HEADS * HEAD_DIM), F32),
        compiler_params=_cparams(("arbitrary",)),
        name="attn_sample",
    )(page_table, *([cache_t] * npg), q, gates, kcn, vcn, state_win_t, kv_new, slope_rows, msel_t)


def _msel_t(n_c_rows, n_s_rows):
    i0 = jnp.arange(n_c_rows)[None, :] * STRIDE
    j0 = jnp.arange(n_s_rows)[:, None] * L_SEL
    return ((i0 < j0 + L_SEL) & (i0 + L_CMP > j0)).astype(BF16)


def _prep_ffn(w_up, w_down):
    d, two_ff = w_up.shape
    ff = two_ff // 2
    n = ff // FF_CHUNK
    assert n * FF_CHUNK == ff
    a = w_up[:, :ff].reshape(d, n, FF_CHUNK)
    b = w_up[:, ff:].reshape(d, n, FF_CHUNK)
    wup_c = jnp.concatenate([a, b], axis=2).transpose(1, 0, 2).astype(BF16)
    wdn_c = w_down.reshape(n, FF_CHUNK, w_down.shape[1]).astype(BF16)
    return wup_c, wdn_c


def _prep_compress(w_cmp1, w_cmp2, pe_cmp):
    hid = w_cmp1.shape[-1]
    wl = jnp.concatenate([w_cmp1[:, :STRIDE], w_cmp1[:, STRIDE:]], axis=-1)
    z = jnp.zeros_like(wl)
    w1 = jnp.concatenate([jnp.concatenate([wl, z], axis=-1), jnp.concatenate([z, wl], axis=-1)], axis=2)
    w1 = w1.reshape(2, STRIDE * LANES, 4 * hid)
    w1f = w_cmp1.reshape(2, L_CMP * HEAD_DIM, hid)
    pe = jnp.zeros((2, 8, L_CMP * HEAD_DIM), F32).at[:, 0].set(pe_cmp.reshape(2, -1))
    w2p = jnp.concatenate([w_cmp2, jnp.zeros_like(w_cmp2)], axis=-1)
    w2n = jnp.zeros((2, N_KV, hid, N_KV * HEAD_DIM), F32)
    for k in range(N_KV):
        w2n = w2n.at[:, k, :, k * HEAD_DIM:(k + 1) * HEAD_DIM].set(w_cmp2)
    return dict(w1=w1.astype(BF16), w1f=w1f.astype(BF16), pe=pe.astype(BF16), w2p=w2p.astype(BF16),
                w2n=w2n.astype(BF16))


def _prep_kv_perm(w_kv):
    d = w_kv.shape[0]
    w6 = w_kv.reshape(d, 6, N_KV, HEAD_DIM)
    z = jnp.zeros((d, N_KV, HEAD_DIM), w_kv.dtype)
    ks = jnp.concatenate([w6[:, 2], z], axis=-1)
    kw = jnp.concatenate([w6[:, 4], z], axis=-1)
    vs = jnp.concatenate([w6[:, 3], z], axis=-1)
    vw = jnp.concatenate([z, w6[:, 5]], axis=-1)
    return jnp.concatenate([ks, kw, vs, vw], axis=1).reshape(d, 4 * N_KV * LANES).astype(BF16)


def _prep_attn_prompt(w_in, w_o):
    d = w_in.shape[0]
    qw = N_HEADS * HEAD_DIM
    scale = HEAD_DIM ** -0.5
    wq = (w_in[:, :qw] * scale).reshape(d, N_HEADS, HEAD_DIM)
    wq = jnp.concatenate([wq, jnp.zeros_like(wq)], axis=-1).reshape(d, N_HEADS * LANES)
    wg = w_in[:, qw:].reshape(d, N_HEADS, 3).transpose(0, 2, 1).reshape(d, 3 * N_HEADS)
    wg = jnp.concatenate([wg, jnp.zeros((d, LANES - 3 * N_HEADS), w_in.dtype)], axis=1)
    w_in_p = jnp.concatenate([wq, wg], axis=1).astype(BF16)
    wo3 = w_o.reshape(N_HEADS, HEAD_DIM, w_o.shape[1])
    wo_p = jnp.concatenate([wo3, wo3], axis=1).reshape(N_HEADS * LANES, w_o.shape[1]).astype(BF16)
    return w_in_p, wo_p


def _prep_attn_sample(w_in, w_o):
    d = w_in.shape[0]
    qw = N_HEADS * HEAD_DIM
    scale = HEAD_DIM ** -0.5
    wq = (w_in[:, :qw] * scale).reshape(d, N_KV, GROUP, HEAD_DIM).transpose(0, 2, 1, 3).reshape(d, qw)
    w_in_s = jnp.concatenate([wq, w_in[:, qw:]], axis=1).astype(BF16)
    wo_s = w_o.reshape(N_KV, GROUP, HEAD_DIM, w_o.shape[1]).transpose(1, 0, 2, 3).reshape(qw, w_o.shape[1])
    return w_in_s, wo_s.astype(BF16)


def kernel(x_prompt, x_sample, cache_kv, page_table, state_win, state_conv, c_prompt, c_sample, w_ada, b_ada, norm_g, conv_w_pw1, conv_b_pw1, conv_w_dw, conv_b_dw, conv_ln_g, conv_ln_b, conv_w_pw2, ffn_w_up, ffn_w_down, kv_norm_g, w_kv, w_cmp1, w_cmp2, pe_cmp, nsa_w_in, nsa_w_o, final_norm_g):
    b, t, d = x_prompt.shape
    bd, tn, _ = x_sample.shape
    depth = norm_g.shape[0]
    n_a = conv_w_pw1.shape[0]
    n_mod = b_ada.shape[0] // d
    past = page_table.shape[1] * PAGE
    hw = N_KV * HEAD_DIM
    assert bd % 8 == 0 and b <= 8 and n_mod == 6 * depth + 4

    c_all = jnp.concatenate([c_sample, c_prompt, jnp.zeros((8 - b % 8, d), F32)], axis=0)
    mods = _mods(c_all, w_ada, b_ada, n_mod)

    ffn_w = [_prep_ffn(ffn_w_up[l], ffn_w_down[l]) for l in range(depth)]
    w_pw1 = conv_w_pw1.astype(BF16)
    w_pw2 = conv_w_pw2.astype(BF16)
    w_kv_t = w_kv.T.astype(BF16)
    w_kv_perm = _prep_kv_perm(w_kv)
    cw = _prep_compress(w_cmp1, w_cmp2, pe_cmp)
    slopes = jnp.exp2(-8.0 * jnp.arange(1, N_HEADS + 1, dtype=F32) / N_HEADS)
    kblk_kv = (6 * depth) // 2
    kblk_final = (6 * depth + 2) // 2

    def trunk(x, time_major, conv_prev_tm, attn_fn_builder):
        conv_states = []
        attn_fn = None
        extra = None
        for l in range(depth):
            if l == n_a:
                attn_fn, extra = attn_fn_builder(x)
            if l < n_a:
                x, st = _conv_layer(x, None if conv_prev_tm is None else conv_prev_tm[l], mods, bd, l, time_major,
                                    norm_g[l, 0], w_pw1[l], conv_b_pw1[l], conv_w_dw[l], conv_b_dw[l],
                                    conv_ln_g[l], conv_ln_b[l], w_pw2[l])
                conv_states.append(st)
                o = wo = None
            else:
                o, wo = attn_fn(x, l)
            last = l == depth - 1
            x = _ffn_layer(x, mods, bd, l, time_major, norm_g[l, 1], ffn_w[l][0], ffn_w[l][1], o=o, wo=wo,
                           final_g=final_norm_g if last else None, n_mod_final=kblk_final)
        return x, extra, conv_states

    def from_feature_major(a, lead):
        nl = len(lead)
        a = a.reshape(lead + (-1, N_KV, HEAD_DIM, a.shape[-1]))
        return a.transpose(tuple(range(nl)) + (nl + 3, nl, nl + 1, nl + 2))

    attn_w_p = [_prep_attn_prompt(nsa_w_in[i], nsa_w_o[i]) for i in range(depth - n_a)]

    def prompt_attn_builder(x):
        kvt_pages, win_t, ks, kw, vs, vw = _kv_proj(x, mods, bd, kblk_kv, False, kv_norm_g, w_kv_t, w_kv_perm)
        ident = jnp.arange(b * t // PAGE, dtype=jnp.int32).reshape(b, t // PAGE)
        kcp, vcp, _, _ = _compress(kvt_pages, ident, cw)
        msel_t = _msel_t(t // STRIDE, t // L_SEL)

        def attn(xl, l):
            w_in_p, wo_p = attn_w_p[l - n_a]
            q, gates = _qin(xl, mods, bd, l, False, norm_g[l, 0], w_in_p)
            return _attn_prompt(slopes, q, gates, ks, kw, vs, vw, kcp, vcp, msel_t), wo_p

        extra = (from_feature_major(kvt_pages, (b * t // PAGE,)), from_feature_major(win_t, (b,)))
        return attn, extra

    y_prompt, (kv_prompt, win_prompt), conv_p = trunk(x_prompt, False, None, prompt_attn_builder)
    conv_prompt = jnp.stack([st[:, 32 - (CONV_W - 1):] for st in conv_p])

    attn_w_s = [_prep_attn_sample(nsa_w_in[i], nsa_w_o[i]) for i in range(depth - n_a)]
    cache_t = cache_kv.transpose(0, 2, 3, 4, 1).reshape(cache_kv.shape[0], 4 * hw, PAGE)
    nbuf = state_win.shape[1]
    state_win_t = state_win.transpose(0, 2, 3, 4, 1).reshape(bd, 2 * hw, nbuf)
    slope_rows = jnp.repeat(slopes, tn).reshape(N_HEADS * tn, 1)

    def sample_attn_builder(x):
        kvt = _kv_proj(x, mods, bd, kblk_kv, True, kv_norm_g, w_kv_t, None)
        kvt3 = kvt.reshape(6 * hw, tn, bd)
        kv_new = kvt3.transpose(2, 1, 0)
        _, _, kcn, vcn = _compress(cache_t, page_table, cw)
        n_s = -(-(past + tn) // L_SEL)
        msel_t = _msel_t(kcn.shape[1], -(-n_s // 8) * 8)

        def attn(xl, l):
            w_in_s, wo_s = attn_w_s[l - n_a]
            a = _qin(xl, mods, bd, l, True, norm_g[l, 0], w_in_s).transpose(1, 0, 2)
            q = a[..., :N_HEADS * HEAD_DIM]
            gates = a[..., N_HEADS * HEAD_DIM:].reshape(bd, tn, N_HEADS, 3).transpose(0, 3, 2, 1)
            gates = gates.reshape(bd, 3, N_HEADS * tn, 1)
            o = _attn_sample(page_table, cache_t, q, gates, kcn, vcn, state_win_t, kv_new, slope_rows, msel_t,
                             past)
            return o.transpose(1, 0, 2), wo_s

        n_keep = min(WINDOW, past + tn)
        win_all_t = jnp.concatenate([state_win_t, kvt3[4 * hw:].transpose(2, 0, 1)], axis=2)[:, :, -n_keep:]
        kv_sample = kvt3[:4 * hw].reshape(4, N_KV, HEAD_DIM, tn, bd).transpose(4, 3, 0, 1, 2)
        return attn, (kv_sample, from_feature_major(win_all_t, (bd,)))

    conv_prev_tm = state_conv.transpose(0, 2, 1, 3)
    y_s_tm, (kv_sample, win_sample), conv_s = trunk(x_sample.transpose(1, 0, 2), True, conv_prev_tm,
                                                    sample_attn_builder)
    y_sample = y_s_tm.transpose(1, 0, 2)
    conv_sample = jnp.stack(conv_s).transpose(0, 2, 1, 3)

    return (y_prompt, y_sample, kv_prompt, kv_sample, win_prompt, win_sample, conv_prompt, conv_sample)
```

```python
import functools

import jax
import jax.numpy as jnp
from jax import lax
from jax.experimental import pallas as pl
from jax.experimental.pallas import tpu as pltpu

F32 = jnp.float32
BF16 = jnp.bfloat16

N_HEADS = 16
N_KV = 4
GROUP = N_HEADS // N_KV
HEAD_DIM = 64
L_CMP = 32
STRIDE = 16
L_SEL = 64
TOP_N = 16
WINDOW = 512
Q_BLK = 128
PAGE = 128
CONV_W = 31
EPS = 1e-6
FORCE = 1e3
NEG = -1e30

LANES = 128
SUBLANES = 8
FF_CHUNK = 256
SEL_TILE = 256
PAGES_PER_STEP = 16
VMEM_LIMIT = 56 * 1024 * 1024


def _cparams(sem):
    return pltpu.CompilerParams(dimension_semantics=sem, vmem_limit_bytes=VMEM_LIMIT)


def _const_spec(shape):
    nd = len(shape)
    return pl.BlockSpec(shape, lambda *_: (0,) * nd, pipeline_mode=pl.Buffered(1))


def _dot(a, b):
    return jnp.dot(a, b, preferred_element_type=F32)


def _dot_nt(a, b):
    return lax.dot_general(a, b, (((1,), (1,)), ((), ())), preferred_element_type=F32)


def _silu(x):
    return x * jax.nn.sigmoid(x)


def _rms_mod(x, g, shift, scale):
    ms = jnp.mean(x * x, axis=-1, keepdims=True)
    y = x * lax.rsqrt(ms + EPS) * g
    return y * (1.0 + scale) + shift


def _mod_getter(m_ref, time_major):
    if time_major:
        return lambda k: m_ref[k][None]
    b = pl.program_id(0)
    return lambda k: m_ref[k, pl.ds(b, 1), :][None]


def _mods_kernel(c_ref, w_ref, b_ref, o_ref):
    c = c_ref[...]
    s = _silu(c).astype(BF16)
    o_ref[0] = _dot(s, w_ref[...].astype(BF16)) + b_ref[0]


def _mods(c_all, w_ada, b_ada, n_mod):
    r, d = c_all.shape
    return pl.pallas_call(
        _mods_kernel,
        grid=(n_mod,),
        in_specs=[pl.BlockSpec((r, d), lambda k: (0, 0)),
                  pl.BlockSpec((d, d), lambda k: (0, k)),
                  pl.BlockSpec((1, 1, d), lambda k: (k, 0, 0))],
        out_specs=pl.BlockSpec((1, r, d), lambda k: (k, 0, 0)),
        out_shape=jax.ShapeDtypeStruct((n_mod, r, d), F32),
        compiler_params=_cparams(("arbitrary",)),
        name="mods",
    )(c_all, w_ada, b_ada.reshape(n_mod, 1, d))


class _Tiling:
    def __init__(self, x_shape, time_major, mods_rows, bk=None, tr_max=512):
        self.time_major = time_major
        if time_major:
            tn, bd, d = x_shape
            bk = min(bd, 32) if bk is None else bk
            assert bd % bk == 0 and bk % 8 == 0
            self.grid = (bd // bk,)
            self.tile = (tn, bk, d)
            self.rows = tn * bk
            self.x_map = lambda j: (0, j, 0)
            self.mod_block = lambda k: (k, bk, d)
            self.mod_map = lambda kblk: (lambda j: (kblk, j, 0))
            self.sem = ("arbitrary",)
        else:
            b, t, d = x_shape
            tr = min(t, tr_max)
            assert t % tr == 0
            self.grid = (b, t // tr)
            self.tile = (1, tr, d)
            self.rows = tr
            self.x_map = lambda bi, i: (bi, i, 0)
            self.mod_block = lambda k: (k, 8, d)
            self.mod_map = lambda kblk: (lambda bi, i: (kblk, mods_rows // 8, 0))
            self.sem = ("arbitrary", "arbitrary")
        self.d = d

    def x_spec(self):
        return pl.BlockSpec(self.tile, self.x_map)

    def mod_spec(self, k, kblk):
        return pl.BlockSpec(self.mod_block(k), self.mod_map(kblk))


def _conv_core(y, ln_g, ln_b):
    mu = jnp.mean(y, axis=-1, keepdims=True)
    yc = y - mu
    var = jnp.mean(yc * yc, axis=-1, keepdims=True)
    return _silu(yc * lax.rsqrt(var + EPS) * ln_g + ln_b)


def _glu_rows(h2, w1_ref, b1_ref, d):
    a1 = _dot(h2, w1_ref[:, :d]) + b1_ref[:, :d]
    a2 = _dot(h2, w1_ref[:, d:]) + b1_ref[:, d:]
    return a1 * jax.nn.sigmoid(a2)


def _conv_prompt_kernel(x_ref, m_ref, g_ref, w1_ref, b1_ref, wdw_ref, bdw_ref, lng_ref, lnb_ref, w2_ref,
                        xo_ref, st_ref, s_ref, p_ref):
    tr, d = x_ref.shape[1], x_ref.shape[2]
    halo = 32
    off = halo - (CONV_W - 1)
    mod = _mod_getter(m_ref, False)

    @pl.when(pl.program_id(1) == 0)
    def _():
        s_ref[0:halo, :] = jnp.zeros((halo, d), F32)
        s_ref[halo + tr:halo + tr + SUBLANES, :] = jnp.zeros((SUBLANES, d), F32)

    x = x_ref[...]
    h = _rms_mod(x, g_ref[...][None], mod(0), mod(1))
    s_ref[halo:halo + tr, :] = _glu_rows(h.reshape(tr, d).astype(BF16), w1_ref, b1_ref, d)

    acc = jnp.broadcast_to(bdw_ref[...], (tr, d))
    for a in range(SUBLANES):
        part = None
        for r in range(a, off + CONV_W, SUBLANES):
            if r >= off:
                term = wdw_ref[r - off:r - off + 1, :] * s_ref[r - a:r - a + tr + SUBLANES, :]
                part = term if part is None else part + term
        if a == 0:
            acc = acc + part[0:tr]
        else:
            p_ref[...] = part
            acc = acc + p_ref[pl.ds(a, tr), :]
    z = _conv_core(acc, lng_ref[...], lnb_ref[...]).astype(BF16)
    y2 = _dot(z, w2_ref[...])
    xo_ref[...] = x + mod(2) * y2[None]
    tail = s_ref[tr:tr + halo, :]
    st_ref[0] = tail
    s_ref[0:halo, :] = tail


def _conv_sample_kernel(x_ref, p_ref, m_ref, g_ref, w1_ref, b1_ref, wdw_ref, bdw_ref, lng_ref, lnb_ref, w2_ref,
                        xo_ref, st_ref, s_ref):
    tn, bk, d = x_ref.shape
    npv = CONV_W - 1
    mod = _mod_getter(m_ref, True)
    x = x_ref[...]
    h = _rms_mod(x, g_ref[...][None], mod(0), mod(1))
    u = _glu_rows(h.reshape(tn * bk, d).astype(BF16), w1_ref, b1_ref, d).reshape(tn, bk, d)
    s_ref[0:npv] = p_ref[...]
    s_ref[npv:npv + tn] = u
    acc = jnp.broadcast_to(bdw_ref[...][None], (tn, bk, d))
    for w in range(CONV_W):
        acc = acc + wdw_ref[w:w + 1, :][None] * s_ref[w:w + tn]
    z = _conv_core(acc, lng_ref[...][None], lnb_ref[...][None]).astype(BF16)
    y2 = _dot(z.reshape(tn * bk, d), w2_ref[...]).reshape(tn, bk, d)
    xo_ref[...] = x + mod(2) * y2
    st_ref[...] = s_ref[tn:tn + npv]


def _conv_layer(x, prev_tm, mods, mods_rows, l, time_major, g1, w1, b1, wdw, bdw, lng, lnb, w2):
    tl = _Tiling(x.shape, time_major, mods_rows)
    d = tl.d
    weights = [g1.reshape(1, d), w1, b1.reshape(1, 2 * d), wdw, bdw.reshape(1, d), lng.reshape(1, d),
               lnb.reshape(1, d), w2]
    w_specs = [_const_spec(w.shape) for w in weights]
    if time_major:
        tn, bd, _ = x.shape
        bk = tl.tile[1]
        npv = CONV_W - 1
        return pl.pallas_call(
            _conv_sample_kernel,
            grid=tl.grid,
            in_specs=[tl.x_spec(), pl.BlockSpec((npv, bk, d), lambda j: (0, j, 0)), tl.mod_spec(6, l)] + w_specs,
            out_specs=[tl.x_spec(), pl.BlockSpec((npv, bk, d), lambda j: (0, j, 0))],
            out_shape=[jax.ShapeDtypeStruct(x.shape, F32), jax.ShapeDtypeStruct((npv, bd, d), F32)],
            scratch_shapes=[pltpu.VMEM((npv + tn, bk, d), F32)],
            compiler_params=_cparams(tl.sem),
            name="conv_sample",
        )(x, prev_tm, mods, *weights)
    b, t, _ = x.shape
    tr = tl.rows
    return pl.pallas_call(
        _conv_prompt_kernel,
        grid=tl.grid,
        in_specs=[tl.x_spec(), tl.mod_spec(6, l)] + w_specs,
        out_specs=[tl.x_spec(), pl.BlockSpec((1, 32, d), lambda bi, i: (bi, 0, 0))],
        out_shape=[jax.ShapeDtypeStruct(x.shape, F32), jax.ShapeDtypeStruct((b, 32, d), F32)],
        scratch_shapes=[pltpu.VMEM((tr + 32 + SUBLANES, d), F32), pltpu.VMEM((tr + SUBLANES, d), F32)],
        compiler_params=_cparams(tl.sem),
        name="conv_prompt",
    )(x, mods, *weights)


def _ffn_kernel(*refs, time_major, has_attn, final, head_major_o):
    it = iter(refs)
    x_ref = next(it)
    m_ref = next(it)
    g2_ref = next(it)
    if has_attn:
        o_ref = next(it)
        wo_ref = next(it)
    wup_ref = next(it)
    wdn_ref = next(it)
    if final:
        fm_ref = next(it)
        fg_ref = next(it)
    out_ref = next(it)
    hb_ref = next(it)
    acc_ref = next(it)

    a, bk, d = x_ref.shape
    rows = a * bk
    mod = _mod_getter(m_ref, time_major)
    x = x_ref[...]
    if has_attn:
        if head_major_o:
            ocat = jnp.concatenate([o_ref[0, h] for h in range(o_ref.shape[1])], axis=1)
        else:
            ocat = o_ref[...].reshape(rows, o_ref.shape[-1]).astype(BF16)
        x = x + mod(2) * _dot(ocat, wo_ref[...]).reshape(a, bk, d)
    h = _rms_mod(x, g2_ref[...][None], mod(3), mod(4))
    hb_ref[...] = h.reshape(rows, d).astype(BF16)
    acc_ref[...] = jnp.zeros((rows, d), F32)
    c = wdn_ref.shape[1]

    def body(j, carry):
        t = _dot(hb_ref[...], wup_ref[j])
        act = (_silu(t[:, :c]) * t[:, c:]).astype(BF16)
        acc_ref[...] += _dot(act, wdn_ref[j])
        return carry

    lax.fori_loop(0, wup_ref.shape[0], body, 0)
    x = x + mod(5) * acc_ref[...].reshape(a, bk, d)
    if final:
        fmod = _mod_getter(fm_ref, time_major)
        x = _rms_mod(x, fg_ref[...][None], fmod(0), fmod(1))
    out_ref[...] = x


def _ffn_layer(x, mods, mods_rows, l, time_major, g2, wup_c, wdn_c, o=None, wo=None, final_g=None, n_mod_final=None):
    has_attn = o is not None
    final = final_g is not None
    tl = _Tiling(x.shape, time_major, mods_rows, tr_max=1024)
    d = tl.d
    args = [x, mods, g2.reshape(1, d)]
    specs = [tl.x_spec(), tl.mod_spec(6, l), _const_spec((1, d))]
    head_major_o = False
    if has_attn:
        if time_major:
            specs.append(pl.BlockSpec(tl.tile[:2] + (o.shape[-1],), tl.x_map))
        else:
            head_major_o = True
            tr = tl.rows
            specs.append(pl.BlockSpec((1, o.shape[1], tr, LANES), lambda bi, i: (bi, 0, i, 0)))
        args += [o, wo]
        specs.append(_const_spec(wo.shape))
    args += [wup_c, wdn_c]
    specs += [_const_spec(wup_c.shape), _const_spec(wdn_c.shape)]
    if final:
        args += [mods, final_g.reshape(1, d)]
        specs += [tl.mod_spec(2, n_mod_final), _const_spec((1, d))]
    kern = functools.partial(_ffn_kernel, time_major=time_major, has_attn=has_attn, final=final,
                             head_major_o=head_major_o)
    return pl.pallas_call(
        kern,
        grid=tl.grid,
        in_specs=specs,
        out_specs=tl.x_spec(),
        out_shape=jax.ShapeDtypeStruct(x.shape, F32),
        scratch_shapes=[pltpu.VMEM((tl.rows, d), BF16), pltpu.VMEM((tl.rows, d), F32)],
        compiler_params=_cparams(tl.sem),
        name="ffn_sample" if time_major else "ffn_prompt",
    )(*args)


def _kv_prompt_kernel(x_ref, m_ref, g_ref, wt_ref, wp_ref, kvt_ref, wint_ref, ks_ref, kw_ref, vs_ref, vw_ref):
    tr, d = x_ref.shape[1], x_ref.shape[2]
    mod = _mod_getter(m_ref, False)
    hk = _rms_mod(x_ref[...], g_ref[...][None], mod(0), mod(1)).reshape(tr, d).astype(BF16)
    kvt = _dot_nt(wt_ref[...], hk)
    n4 = kvt_ref.shape[1]
    for p in range(tr // PAGE):
        kvt_ref[p] = kvt[:n4, p * PAGE:(p + 1) * PAGE]
    wint_ref[0] = kvt[n4:, :]
    att = _dot(hk, wp_ref[...])
    t0 = pl.program_id(1) * tr
    blk = (t0 + lax.broadcasted_iota(jnp.int32, (tr, LANES), 0)) // L_SEL
    lane = lax.broadcasted_iota(jnp.int32, (tr, LANES), 1)
    onehot = lane - HEAD_DIM == blk
    low = lane < HEAD_DIM
    for k in range(N_KV):
        def part(j):
            return att[:, (j * N_KV + k) * LANES:(j * N_KV + k + 1) * LANES]
        ks_ref[0, k] = jnp.where(onehot, 1.0, part(0)).astype(BF16)
        kw_ref[0, k] = part(1).astype(BF16)
        vs_ref[0, k] = jnp.where(low, part(2), 1.0).astype(BF16)
        vw_ref[0, k] = jnp.where(low, 1.0, part(3)).astype(BF16)


def _kv_sample_kernel(x_ref, m_ref, g_ref, wt_ref, kvt_ref):
    tn, bk, d = x_ref.shape
    mod = _mod_getter(m_ref, True)
    hk = _rms_mod(x_ref[...], g_ref[...][None], mod(0), mod(1)).reshape(tn * bk, d).astype(BF16)
    kvt_ref[...] = _dot_nt(wt_ref[...], hk)


def _kv_proj(x, mods, mods_rows, kblk, time_major, g, w_t, w_perm):
    d = x.shape[-1]
    nkv = w_t.shape[0]
    if time_major:
        tn, bd, _ = x.shape
        tl = _Tiling(x.shape, True, mods_rows, bk=bd)
        return pl.pallas_call(
            _kv_sample_kernel,
            grid=tl.grid,
            in_specs=[tl.x_spec(), tl.mod_spec(2, kblk), _const_spec((1, d)), _const_spec(w_t.shape)],
            out_specs=pl.BlockSpec((nkv, tn * bd), lambda j: (0, 0)),
            out_shape=jax.ShapeDtypeStruct((nkv, tn * bd), F32),
            compiler_params=_cparams(tl.sem),
            name="kv_sample",
        )(x, mods, g.reshape(1, d), w_t)
    tl = _Tiling(x.shape, False, mods_rows)
    b, t, _ = x.shape
    tr = tl.rows
    n4 = 4 * N_KV * HEAD_DIM
    nw = min(WINDOW, t)
    assert tr == nw and tr % PAGE == 0
    ppt = tr // PAGE
    hm = pl.BlockSpec((1, N_KV, tr, LANES), lambda bi, i: (bi, 0, i, 0))
    hm_shape = jax.ShapeDtypeStruct((b, N_KV, t, LANES), BF16)
    return pl.pallas_call(
        _kv_prompt_kernel,
        grid=tl.grid,
        in_specs=[tl.x_spec(), tl.mod_spec(2, kblk), _const_spec((1, d)), _const_spec(w_t.shape),
                  _const_spec(w_perm.shape)],
        out_specs=[pl.BlockSpec((ppt, n4, PAGE), lambda bi, i: (bi * (t // tr) + i, 0, 0)),
                   pl.BlockSpec((1, nkv - n4, nw), lambda bi, i: (bi, 0, 0)), hm, hm, hm, hm],
        out_shape=[jax.ShapeDtypeStruct((b * t // PAGE, n4, PAGE), F32),
                   jax.ShapeDtypeStruct((b, nkv - n4, nw), F32), hm_shape, hm_shape, hm_shape, hm_shape],
        compiler_params=_cparams(tl.sem),
        name="kv_prompt",
    )(x, mods, g.reshape(1, d), w_t, w_perm)


def _compress_kernel(pt_ref, *refs, spb):
    del pt_ref
    n_pages = spb * PAGES_PER_STEP
    pages = refs[:n_pages]
    w1_ref, w1f_ref, pe_ref, w2p_ref, w2n_ref = refs[n_pages:n_pages + 5]
    kcp_ref, vcp_ref, kcn_ref, vcn_ref = refs[n_pages + 5:n_pages + 9]
    x_ref, hs_ref = refs[n_pages + 9:]
    nh = x_ref.shape[1] // spb
    rows_all = spb * nh
    hid = w2p_ref.shape[1]
    per_page = PAGE // STRIDE
    pg0 = pl.program_id(1) * PAGES_PER_STEP

    pi = lax.broadcasted_iota(jnp.int32, (PAGE, PAGE), 0)
    pj = lax.broadcasted_iota(jnp.int32, (PAGE, PAGE), 1)
    perm = jnp.where(pj == (pi % per_page) * STRIDE + pi // per_page, 1.0, 0.0).astype(BF16)
    for s in range(spb):
        for pg in range(PAGES_PER_STEP):
            r0 = pl.multiple_of(s * nh + (pg0 + pg) * per_page, per_page)
            rp = _dot_nt(perm, pages[s * PAGES_PER_STEP + pg][0].astype(BF16))
            for l in range(STRIDE):
                x_ref[l, pl.ds(r0, per_page), :] = rp[l * per_page:(l + 1) * per_page, :]

    @pl.when(pl.program_id(1) == pl.num_programs(1) - 1)
    def _():
        rows = lax.broadcasted_iota(jnp.int32, (rows_all, hid), 0) % nh
        hs_ref[rows_all:rows_all + 8, :] = jnp.zeros((8, hid), F32)
        for c, (pad_ref, nat_ref) in enumerate(((kcp_ref, kcn_ref), (vcp_ref, vcn_ref))):
            cvec = _dot(pe_ref[c], w1f_ref[c])[0:1, :]
            nat = jnp.zeros((rows_all, N_KV * HEAD_DIM), F32)
            for pr in range(N_KV // 2):
                lo = c * N_KV * HEAD_DIM + pr * LANES
                xcat = jnp.concatenate([x_ref[l, :, lo:lo + LANES] for l in range(STRIDE)], axis=1)
                acc = _dot(xcat.astype(BF16), w1_ref[c])
                for e in range(2):
                    k = 2 * pr + e
                    hs_ref[0:rows_all, :] = acc[:, (2 * e + 1) * hid:(2 * e + 2) * hid]
                    hsum = acc[:, 2 * e * hid:(2 * e + 1) * hid] + hs_ref[pl.ds(1, rows_all), :] + cvec
                    hb = jnp.where(rows < nh - 1, _silu(hsum), 0.0).astype(BF16)
                    padded = _dot(hb, w2p_ref[c]).astype(BF16)
                    for s in range(spb):
                        pad_ref[s, k] = padded[s * nh:(s + 1) * nh]
                    nat = nat + _dot(hb, w2n_ref[c, k])
            for s in range(spb):
                nat_ref[s] = nat[s * nh:(s + 1) * nh].astype(BF16)


def _compress(pages, page_table, cw):
    bs, p = page_table.shape
    assert p % PAGES_PER_STEP == 0
    ng = p // PAGES_PER_STEP
    nh = p * (PAGE // STRIDE)
    wcols = 2 * N_KV * HEAD_DIM
    hid = cw["w2p"].shape[1]
    spb = 2 if bs % 2 == 0 else 1

    def page_spec(s, pg):
        return pl.BlockSpec((1, wcols, PAGE), lambda b, g, pt: (pt[b * spb + s, g * PAGES_PER_STEP + pg], 0, 0))

    def full(a):
        nd = a.ndim
        return pl.BlockSpec(a.shape, lambda b, g, pt: (0,) * nd)

    consts = [cw["w1"], cw["w1f"], cw["pe"], cw["w2p"], cw["w2n"]]
    pad_shape = jax.ShapeDtypeStruct((bs, N_KV, nh, LANES), BF16)
    nat_shape = jax.ShapeDtypeStruct((bs, nh, N_KV * HEAD_DIM), BF16)
    pad_spec = pl.BlockSpec((spb, N_KV, nh, LANES), lambda b, g, pt: (b, 0, 0, 0))
    nat_spec = pl.BlockSpec((spb, nh, N_KV * HEAD_DIM), lambda b, g, pt: (b, 0, 0))
    grid_spec = pltpu.PrefetchScalarGridSpec(
        num_scalar_prefetch=1,
        grid=(bs // spb, ng),
        in_specs=[page_spec(s, pg) for s in range(spb) for pg in range(PAGES_PER_STEP)] + [full(a) for a in consts],
        out_specs=[pad_spec, pad_spec, nat_spec, nat_spec],
        scratch_shapes=[pltpu.VMEM((STRIDE, spb * nh, wcols), F32), pltpu.VMEM((spb * nh + 8, hid), F32)],
    )
    return pl.pallas_call(
        functools.partial(_compress_kernel, spb=spb),
        grid_spec=grid_spec,
        out_shape=[pad_shape, pad_shape, nat_shape, nat_shape],
        compiler_params=_cparams(("arbitrary", "arbitrary")),
        name="compress",
    )(page_table, *([pages] * (spb * PAGES_PER_STEP)), *consts)


def _qin_prompt_kernel(x_ref, m_ref, g_ref, w_ref, q_ref, gate_ref):
    tr, d = x_ref.shape[1], x_ref.shape[2]
    mod = _mod_getter(m_ref, False)
    h = _rms_mod(x_ref[...], g_ref[...][None], mod(0), mod(1)).reshape(tr, d).astype(BF16)
    a = _dot(h, w_ref[...])
    for hd in range(N_HEADS):
        q_ref[0, hd] = a[:, hd * LANES:(hd + 1) * LANES].astype(BF16)
    gate_ref[0] = a[:, N_HEADS * LANES:]


def _qin_sample_kernel(x_ref, m_ref, g_ref, w_ref, a_ref):
    tn, bk, d = x_ref.shape
    mod = _mod_getter(m_ref, True)
    h = _rms_mod(x_ref[...], g_ref[...][None], mod(0), mod(1)).reshape(tn * bk, d).astype(BF16)
    a_ref[...] = _dot(h, w_ref[...]).reshape(tn, bk, a_ref.shape[-1])


def _qin(x, mods, mods_rows, l, time_major, g1, w):
    tl = _Tiling(x.shape, time_major, mods_rows)
    d = tl.d
    n = w.shape[1]
    common = dict(grid=tl.grid, compiler_params=_cparams(tl.sem))
    in_specs = [tl.x_spec(), tl.mod_spec(6, l), _const_spec((1, d)), _const_spec(w.shape)]
    if time_major:
        return pl.pallas_call(
            _qin_sample_kernel, in_specs=in_specs,
            out_specs=pl.BlockSpec(tl.tile[:2] + (n,), tl.x_map),
            out_shape=jax.ShapeDtypeStruct(x.shape[:2] + (n,), F32),
            name="qin_sample", **common)(x, mods, g1.reshape(1, d), w)
    b, t, _ = x.shape
    tr = tl.rows
    return pl.pallas_call(
        _qin_prompt_kernel, in_specs=in_specs,
        out_specs=[pl.BlockSpec((1, N_HEADS, tr, LANES), lambda bi, i: (bi, 0, i, 0)),
                   pl.BlockSpec((1, tr, LANES), tl.x_map)],
        out_shape=[jax.ShapeDtypeStruct((b, N_HEADS, t, LANES), BF16), jax.ShapeDtypeStruct((b, t, LANES), F32)],
        name="qin_prompt", **common)(x, mods, g1.reshape(1, d), w)


def _block_rank(score_ref, ns):
    score = score_ref[...]
    blk = lax.broadcasted_iota(jnp.int32, score.shape, 0)
    rank = jnp.zeros(score.shape, F32)
    for i in range(ns):
        si = score_ref[i:i + 1, :]
        rank = rank + jnp.where(blk > i, jnp.where(si >= score, 1.0, 0.0), jnp.where(si > score, 1.0, 0.0))
    return rank


def _add_alibi(s, slopes_ref, head0, pos_f, qb):
    return jnp.concatenate([s[g * qb:(g + 1) * qb] + slopes_ref[head0 + g] * pos_f for g in range(GROUP)], axis=0)


def _attn_prompt_kernel(slopes_ref, q_ref, gate_ref, ks_ref, kw_ref, vs_ref, vw_ref, kc_ref, vc_ref, msel_ref,
                        ega_ref, egb_ref, o_ref,
                        s_ref, sc_ref, mt_ref, qa_ref, oc_ref, as_ref, mx_ref, ga_ref, gb_ref, ids_ref, cnt_ref):
    qb = q_ref.shape[2]
    rows = GROUP * qb
    t = ks_ref.shape[2]
    nc = kc_ref.shape[2]
    ns = msel_ref.shape[0]
    max_tiles = t // SEL_TILE
    blocks_per_tile = SEL_TILE // L_SEL
    i = pl.program_id(1)
    s0 = i * qb
    n_tiles = (s0 + qb + SEL_TILE - 1) // SEL_TILE
    qpos4 = s0 + lax.broadcasted_iota(jnp.int32, (rows, 1), 0) % qb
    low4 = lax.broadcasted_iota(jnp.int32, (rows, LANES), 1) < HEAD_DIM
    low1 = lax.broadcasted_iota(jnp.int32, (qb, LANES), 1) < HEAD_DIM

    gs = jax.nn.sigmoid(gate_ref[0])
    g_hi = gs.astype(BF16)
    g_lo = (gs - g_hi.astype(F32)).astype(BF16)
    g2 = jnp.concatenate([g_hi, g_lo], axis=1)
    ga = _dot(g2, ega_ref[...])
    gb = _dot(g2, egb_ref[...])
    for h in range(N_HEADS):
        ga_ref[h] = ga[:, h * LANES:(h + 1) * LANES]
        gb_ref[h] = gb[:, h * LANES:(h + 1) * LANES]

    cidx = lax.broadcasted_iota(jnp.int32, (1, nc), 1)
    cend = cidx * STRIDE + (L_CMP - 1)
    cmask = (cend <= qpos4) & (cidx < nc - 1)
    cend_f = cend.astype(F32)
    row_ok = qpos4 >= L_CMP - 1
    blk = lax.broadcasted_iota(jnp.int32, (ns, qb), 0)
    qpos_t = s0 + lax.broadcasted_iota(jnp.int32, (ns, qb), 1)
    cur = qpos_t // L_SEL
    valid_t = blk * L_SEL <= qpos_t
    forced = (blk == 0) | (blk == cur) | (blk == cur - 1)
    tile_of_blk = jnp.where(lax.broadcasted_iota(jnp.int32, (max_tiles, ns), 1) // blocks_per_tile
                            == lax.broadcasted_iota(jnp.int32, (max_tiles, ns), 0), 1.0, 0.0).astype(BF16)

    for kvh in range(N_KV):
        q4 = q_ref[0, kvh * GROUP:(kvh + 1) * GROUP].reshape(rows, LANES)
        sc = _add_alibi(_dot_nt(q4, kc_ref[0, kvh]), slopes_ref, kvh * GROUP, cend_f, qb)
        sc = jnp.where(cmask, sc, NEG)
        e = jnp.exp(sc - jnp.max(sc, axis=1, keepdims=True))
        r = jnp.where(row_ok, 1.0 / jnp.sum(e, axis=1, keepdims=True), 0.0)
        pb = (e * r).astype(BF16)
        oc_ref[kvh] = _dot(pb, vc_ref[0, kvh])
        imp4 = _dot_nt(msel_ref[...], pb)
        imp_t = imp4[:, 0:qb]
        for g in range(1, GROUP):
            imp_t = imp_t + imp4[:, g * qb:(g + 1) * qb]
        sc_ref[kvh] = jnp.where(valid_t, imp_t + jnp.where(forced, FORCE, 0.0), -jnp.inf)
        keep = (_block_rank(sc_ref.at[kvh], ns) < TOP_N) & valid_t
        mt_ref[kvh] = jnp.zeros(mt_ref.shape[1:], F32)
        mt_ref[kvh, HEAD_DIM:HEAD_DIM + ns, :] = jnp.where(keep, 0.0, NEG)
        mask_b = mt_ref[kvh].T.astype(BF16)
        qa_ref[kvh] = jnp.where(low4, q4, jnp.concatenate([mask_b] * GROUP, axis=0))

        per_tile = jnp.sum(_dot(tile_of_blk, jnp.where(keep, 1.0, 0.0).astype(BF16)), axis=1, keepdims=True)
        for pos in range(ids_ref.shape[1]):
            ids_ref[kvh, pos] = jnp.int32(-1)
        n_act = jnp.int32(0)
        for kt in range(max_tiles - 1):
            active = jnp.logical_and(kt < n_tiles - 1, per_tile[kt, 0] > 0.5)
            ids_ref[kvh, n_act] = jnp.where(active, jnp.int32(kt), jnp.int32(-1))
            n_act = n_act + active.astype(jnp.int32)
        ids_ref[kvh, n_act] = n_tiles - 1
        cnt_ref[kvh] = n_act + 1

    lane_blocks = SEL_TILE // LANES
    for pair in range(N_KV // 2):
        kvs = (2 * pair, 2 * pair + 1)
        n_steps = (jnp.maximum(cnt_ref[kvs[0]], cnt_ref[kvs[1]]) + 1) // 2
        for kvh in kvs:
            mx_ref[kvh] = jnp.full((rows, LANES), NEG, F32)
            as_ref[kvh] = jnp.zeros((rows, LANES), F32)

        def entries(step):
            out = []
            for u in range(2):
                for j, kvh in enumerate(kvs):
                    kt = ids_ref[kvh, 2 * step + u]
                    null = kt < 0
                    out.append((j, kvh, jnp.maximum(kt, 0), jnp.where(null, max_tiles, kt), null))
            return out

        def score_step(step, carry):
            for j, kvh, kt, slot, null in entries(step):
                k0 = pl.multiple_of(kt * SEL_TILE, SEL_TILE)
                kpos = k0 + lax.broadcasted_iota(jnp.int32, (1, SEL_TILE), 1)
                s = _dot_nt(qa_ref[kvh], ks_ref[0, kvh, pl.ds(k0, SEL_TILE), :])
                s = _add_alibi(s, slopes_ref, kvh * GROUP, kpos.astype(F32), qb)
                s = jnp.where(kpos + jnp.where(null, jnp.int32(1 << 30), 0) <= qpos4, s, NEG)
                s_ref[j, slot] = s
                m = s[:, 0:LANES]
                for c in range(1, lane_blocks):
                    m = jnp.maximum(m, s[:, c * LANES:(c + 1) * LANES])
                mx_ref[kvh] = jnp.maximum(mx_ref[kvh], m)
            return carry

        lax.fori_loop(0, n_steps, score_step, 0)
        mrow = [jnp.max(mx_ref[kvh], axis=1, keepdims=True) for kvh in kvs]

        def weight_step(step, carry):
            for j, kvh, kt, slot, null in entries(step):
                k0 = pl.multiple_of(kt * SEL_TILE, SEL_TILE)
                p = jnp.exp(s_ref[j, slot] - mrow[j]).astype(BF16)
                as_ref[kvh] += _dot(p, vs_ref[0, kvh, pl.ds(k0, SEL_TILE), :])
            return carry

        lax.fori_loop(0, n_steps, weight_step, 0)

    wlen = WINDOW + qb
    w0 = pl.multiple_of(jnp.maximum(s0 - WINDOW, 0), qb)
    wpos = w0 + lax.broadcasted_iota(jnp.int32, (1, wlen), 1)
    wpos_f = wpos.astype(F32)
    dist = qpos4 - wpos
    wmask = (dist >= 0) & (dist < WINDOW)
    for kvh in range(N_KV):
        q4 = q_ref[0, kvh * GROUP:(kvh + 1) * GROUP].reshape(rows, LANES)
        s = _add_alibi(_dot_nt(q4, kw_ref[0, kvh, pl.ds(w0, wlen), :]), slopes_ref, kvh * GROUP, wpos_f, qb)
        s = jnp.where(wmask, s, NEG)
        p = jnp.exp(s - jnp.max(s, axis=1, keepdims=True)).astype(BF16)
        acc_w = _dot(p, vw_ref[0, kvh, pl.ds(w0, wlen), :])
        acc_s = as_ref[kvh]
        o_s = acc_s * (1.0 / pltpu.roll(acc_s, HEAD_DIM, 1))
        o_w = acc_w * (1.0 / pltpu.roll(acc_w, HEAD_DIM, 1))
        ga4 = ga_ref[kvh * GROUP:(kvh + 1) * GROUP].reshape(rows, LANES)
        gb4 = gb_ref[kvh * GROUP:(kvh + 1) * GROUP].reshape(rows, LANES)
        out = ga4 * jnp.where(low4, oc_ref[kvh], o_w) + gb4 * jnp.where(low4, o_s, 0.0)
        out = out + pltpu.roll(out, HEAD_DIM, 1)
        for pr in range(GROUP // 2):
            even = out[(2 * pr) * qb:(2 * pr + 1) * qb]
            odd = out[(2 * pr + 1) * qb:(2 * pr + 2) * qb]
            o_ref[0, kvh * (GROUP // 2) + pr] = jnp.where(low1, even, odd).astype(BF16)


def _gate_expand():
    src = jnp.arange(2 * LANES) % LANES
    br = (src // N_HEADS)[:, None]
    hd = (src % N_HEADS)[:, None]
    col = jnp.arange(N_HEADS * LANES)
    same = hd == (col // LANES)[None, :]
    low = (col % LANES < HEAD_DIM)[None, :]
    ea = same & (((br == 0) & low) | ((br == 2) & ~low))
    eb = same & (br == 1) & low
    return ea.astype(BF16), eb.astype(BF16)


def _attn_prompt(slopes, q, gates, ks, kw, vs, vw, kcp, vcp, msel_t):
    b, _, t, _ = q.shape
    nc = kcp.shape[2]
    ns = msel_t.shape[0]
    assert t % SEL_TILE == 0 and t >= WINDOW + Q_BLK and ns <= HEAD_DIM and ns % 8 == 0
    ea, eb = _gate_expand()
    rows = GROUP * Q_BLK
    max_tiles = t // SEL_TILE
    qspec = pl.BlockSpec((1, N_HEADS, Q_BLK, LANES), lambda bi, i: (bi, 0, i, 0))
    kvspec = pl.BlockSpec((1, N_KV, t, LANES), lambda bi, i: (bi, 0, 0, 0), pipeline_mode=pl.Buffered(1))
    cspec = pl.BlockSpec((1, N_KV, nc, LANES), lambda bi, i: (bi, 0, 0, 0))
    return pl.pallas_call(
        _attn_prompt_kernel,
        grid=(b, t // Q_BLK),
        in_specs=[pl.BlockSpec(memory_space=pltpu.SMEM), qspec,
                  pl.BlockSpec((1, Q_BLK, LANES), lambda bi, i: (bi, i, 0)),
                  kvspec, kvspec, kvspec, kvspec, cspec, cspec,
                  _const_spec(msel_t.shape), _const_spec(ea.shape), _const_spec(eb.shape)],
        out_specs=pl.BlockSpec((1, N_HEADS // 2, Q_BLK, LANES), lambda bi, i: (bi, 0, i, 0)),
        out_shape=jax.ShapeDtypeStruct((b, N_HEADS // 2, t, LANES), BF16),
        scratch_shapes=[pltpu.VMEM((2, max_tiles + 1, rows, SEL_TILE), F32),
                        pltpu.VMEM((N_KV, ns, Q_BLK), F32), pltpu.VMEM((N_KV, LANES, Q_BLK), F32),
                        pltpu.VMEM((N_KV, rows, LANES), BF16), pltpu.VMEM((N_KV, rows, LANES), F32),
                        pltpu.VMEM((N_KV, rows, LANES), F32), pltpu.VMEM((N_KV, rows, LANES), F32),
                        pltpu.VMEM((N_HEADS, Q_BLK, LANES), F32), pltpu.VMEM((N_HEADS, Q_BLK, LANES), F32),
                        pltpu.SMEM((N_KV, max_tiles + 2), jnp.int32), pltpu.SMEM((N_KV,), jnp.int32)],
        compiler_params=_cparams(("arbitrary", "arbitrary")),
        name="attn_prompt",
    )(slopes, q, gates, ks, kw, vs, vw, kcp, vcp, msel_t, ea, eb)


def _attn_sample_kernel(pt_ref, *refs, past, spb):
    del pt_ref
    npg = past // PAGE
    for seq in range(spb):
        _attn_sample_seq(seq, refs[seq * npg:(seq + 1) * npg], *refs[spb * npg:], past=past)


def _attn_sample_seq(seq, pages, q_ref, gate_ref, kc_ref, vc_ref, sw_ref, kvn_ref, slope_ref, msel_ref, o_ref,
                     s_ref, sc_ref, st_ref, *, past):
    npg = past // PAGE
    s_ref, sc_ref, st_ref = s_ref.at[seq], sc_ref.at[seq], st_ref.at[seq]
    tn = q_ref.shape[1]
    hw = N_KV * HEAD_DIM
    rows = N_KV * GROUP * tn
    nc = kc_ref.shape[1]
    ns_pad = sc_ref.shape[0]
    nblk = npg + 1
    n_s = (past + tn + L_SEL - 1) // L_SEL
    f_sel = 2 * hw
    slope = slope_ref[...]

    rr = lax.broadcasted_iota(jnp.int32, (rows, 1), 0)
    qpos = past + rr % tn
    col = lax.broadcasted_iota(jnp.int32, (tn, hw), 1) // HEAD_DIM
    colr = lax.broadcasted_iota(jnp.int32, (rows, hw), 1) // HEAD_DIM
    rowk = lax.broadcasted_iota(jnp.int32, (rows, hw), 0) // (GROUP * tn)
    diag = colr == rowk

    qs = q_ref[seq]
    pieces = []
    for k in range(N_KV):
        for g in range(GROUP):
            pieces.append(jnp.where(col == k, qs[:, g * hw:(g + 1) * hw], 0.0))
    qbd = jnp.concatenate(pieces, axis=0).astype(BF16)

    cidx = lax.broadcasted_iota(jnp.int32, (1, nc), 1)
    cend = cidx * STRIDE + (L_CMP - 1)
    cvalid = (cend <= qpos) & (cidx < nc - 1)
    sc = jnp.where(cvalid, _dot_nt(qbd, kc_ref[seq]) + slope * cend.astype(F32), NEG)
    e = jnp.exp(sc - jnp.max(sc, axis=1, keepdims=True))
    r = jnp.where(qpos >= L_CMP - 1, 1.0 / jnp.sum(e, axis=1, keepdims=True), 0.0)
    pc = (e * r).astype(BF16)
    o_c = _dot(pc, vc_ref[seq])

    imp = _dot_nt(msel_ref[...], pc)
    tot = imp
    for g in range(1, GROUP):
        tot = tot + pltpu.roll(imp, rows - g * tn, 1)
    blk = lax.broadcasted_iota(jnp.int32, (ns_pad, rows), 0)
    lane_r = lax.broadcasted_iota(jnp.int32, (ns_pad, rows), 1)
    qpos_l = past + lane_r % tn
    cur = qpos_l // L_SEL
    valid_l = (blk * L_SEL <= qpos_l) & (blk < n_s)
    forced = (blk == 0) | (blk == cur) | (blk == cur - 1)
    sc_ref[...] = jnp.where(valid_l, tot + jnp.where(forced, FORCE, 0.0), -jnp.inf)
    keep = (_block_rank(sc_ref, n_s) < TOP_N) & valid_l
    lead = (lane_r // tn) % GROUP == 0
    keep_f = jnp.where(keep & lead, 1.0, 0.0)
    spread = keep_f
    for g in range(1, GROUP):
        spread = spread + pltpu.roll(keep_f, g * tn, 1)
    st_ref[...] = jnp.zeros(st_ref.shape, F32)
    st_ref[0:ns_pad, :] = spread
    keep_rows = st_ref[...].T.astype(BF16)

    zpad = jnp.zeros((PAGE - tn, hw), F32)
    bsel = lax.broadcasted_iota(jnp.int32, (LANES, PAGE), 0)
    ksub = lax.broadcasted_iota(jnp.int32, (LANES, PAGE), 1) // L_SEL
    kloc = lax.broadcasted_iota(jnp.int32, (1, PAGE), 1)
    mrun = jnp.full((rows, 1), NEG, F32)
    for j in range(nblk):
        if j < npg:
            qk = _dot(qbd, pages[j][0, 0:hw, :].astype(BF16))
        else:
            kb = jnp.concatenate([kvn_ref[seq, :,f_sel:f_sel + hw], zpad], axis=0).astype(BF16)
            qk = _dot_nt(qbd, kb)
        kpos = j * PAGE + kloc
        expand = jnp.where(bsel == (PAGE // L_SEL) * j + ksub, 1.0, 0.0).astype(BF16)
        kept = _dot(keep_rows, expand)
        s = jnp.where((kept > 0.5) & (kpos <= qpos), qk + slope * kpos.astype(F32), NEG)
        s_ref[:, j * PAGE:(j + 1) * PAGE] = s
        mrun = jnp.maximum(mrun, jnp.max(s, axis=1, keepdims=True))
    lsum = jnp.zeros((rows, 1), F32)
    o_s = jnp.zeros((rows, hw), F32)
    for j in range(nblk):
        p = jnp.exp(s_ref[:, j * PAGE:(j + 1) * PAGE] - mrun)
        lsum = lsum + jnp.sum(p, axis=1, keepdims=True)
        if j < npg:
            o_s = o_s + _dot_nt(p.astype(BF16), pages[j][0, hw:2 * hw, :].astype(BF16))
        else:
            vb = jnp.concatenate([kvn_ref[seq, :,f_sel + hw:f_sel + 2 * hw], zpad], axis=0).astype(BF16)
            o_s = o_s + _dot(p.astype(BF16), vb)
    o_s = o_s * (1.0 / lsum)

    nbuf = sw_ref.shape[2]
    f_win = 4 * hw
    kwn = jnp.concatenate([kvn_ref[seq, :,f_win:f_win + hw], zpad], axis=0).astype(BF16)
    vwn = jnp.concatenate([kvn_ref[seq, :,f_win + hw:f_win + 2 * hw], zpad], axis=0).astype(BF16)
    pos_a = past - nbuf + lax.broadcasted_iota(jnp.int32, (1, nbuf), 1)
    pos_b = past + kloc
    da = qpos - pos_a
    db = qpos - pos_b
    va = (da >= 0) & (da < WINDOW) & (pos_a >= 0)
    vb_ok = (db >= 0) & (db < WINDOW)
    s_a = jnp.where(va, _dot(qbd, sw_ref[seq,0:hw, :].astype(BF16)) + slope * pos_a.astype(F32), NEG)
    s_b = jnp.where(vb_ok, _dot_nt(qbd, kwn) + slope * pos_b.astype(F32), NEG)
    m = jnp.maximum(jnp.max(s_a, axis=1, keepdims=True), jnp.max(s_b, axis=1, keepdims=True))
    p_a = jnp.exp(s_a - m)
    p_b = jnp.exp(s_b - m)
    l_w = jnp.sum(p_a, axis=1, keepdims=True) + jnp.sum(p_b, axis=1, keepdims=True)
    o_w = (_dot_nt(p_a.astype(BF16), sw_ref[seq,hw:2 * hw, :].astype(BF16)) + _dot(p_b.astype(BF16), vwn)) * (1.0 / l_w)

    gt = jax.nn.sigmoid(gate_ref[seq])
    tot_o = jnp.where(diag, gt[0] * o_c + gt[1] * o_s + gt[2] * o_w, 0.0)
    for g in range(GROUP):
        acc = jnp.zeros((tn, hw), F32)
        for k in range(N_KV):
            r0 = (k * GROUP + g) * tn
            acc = acc + tot_o[r0:r0 + tn, :]
        o_ref[seq, :, g * hw:(g + 1) * hw] = acc


def _attn_sample(page_table, cache_t, q, gates, kcn, vcn, state_win_t, kv_new, slope_rows, msel_t, past):
    bd, tn, qw = q.shape
    npg = past // PAGE
    hw = N_KV * HEAD_DIM
    rows = N_KV * GROUP * tn
    assert rows == LANES and msel_t.shape[1] == kcn.shape[1]
    ns_pad = msel_t.shape[0]
    spb = 2 if bd % 2 == 0 else 1

    def page_spec(s, pg):
        return pl.BlockSpec((1, 2 * hw, PAGE), lambda b, pt: (pt[b * spb + s, pg], 1, 0))

    def per_seq(a):
        nd = a.ndim
        return pl.BlockSpec((spb,) + a.shape[1:], lambda b, pt: (b,) + (0,) * (nd - 1))

    def full(a):
        nd = a.ndim
        return pl.BlockSpec(a.shape, lambda b, pt: (0,) * nd)

    grid_spec = pltpu.PrefetchScalarGridSpec(
        num_scalar_prefetch=1,
        grid=(bd // spb,),
        in_specs=[page_spec(s, pg) for s in range(spb) for pg in range(npg)]
        + [per_seq(q), per_seq(gates), per_seq(kcn), per_seq(vcn), per_seq(state_win_t), per_seq(kv_new),
           full(slope_rows), full(msel_t)],
        out_specs=pl.BlockSpec((spb, tn, N_HEADS * HEAD_DIM), lambda b, pt: (b, 0, 0)),
        scratch_shapes=[pltpu.VMEM((spb, rows, (npg + 1) * PAGE), F32), pltpu.VMEM((spb, ns_pad, rows), F32),
                        pltpu.VMEM((spb, LANES, rows), F32)],
    )
    return pl.pallas_call(
        functools.partial(_attn_sample_kernel, past=past, spb=spb),
        grid_spec=grid_spec,
        out_shape=jax.ShapeDtypeStruct((bd, tn, N_HEADS * HEAD_DIM), F32),
        compiler_params=_cparams(("arbitrary",)),
        name="attn_sample",
    )(page_table, *([cache_t] * (spb * npg)), q, gates, kcn, vcn, state_win_t, kv_new, slope_rows, msel_t)


def _msel_t(n_c_rows, n_s_rows):
    i0 = jnp.arange(n_c_rows)[None, :] * STRIDE
    j0 = jnp.arange(n_s_rows)[:, None] * L_SEL
    return ((i0 < j0 + L_SEL) & (i0 + L_CMP > j0)).astype(BF16)


def _prep_ffn(w_up, w_down):
    d, two_ff = w_up.shape
    ff = two_ff // 2
    n = ff // FF_CHUNK
    assert n * FF_CHUNK == ff
    a = w_up[:, :ff].reshape(d, n, FF_CHUNK)
    b = w_up[:, ff:].reshape(d, n, FF_CHUNK)
    wup_c = jnp.concatenate([a, b], axis=2).transpose(1, 0, 2).astype(BF16)
    wdn_c = w_down.reshape(n, FF_CHUNK, w_down.shape[1]).astype(BF16)
    return wup_c, wdn_c


def _prep_compress(w_cmp1, w_cmp2, pe_cmp):
    hid = w_cmp1.shape[-1]
    wl = jnp.concatenate([w_cmp1[:, :STRIDE], w_cmp1[:, STRIDE:]], axis=-1)
    z = jnp.zeros_like(wl)
    w1 = jnp.concatenate([jnp.concatenate([wl, z], axis=-1), jnp.concatenate([z, wl], axis=-1)], axis=2)
    w1 = w1.reshape(2, STRIDE * LANES, 4 * hid)
    w1f = w_cmp1.reshape(2, L_CMP * HEAD_DIM, hid)
    pe = jnp.zeros((2, 8, L_CMP * HEAD_DIM), F32).at[:, 0].set(pe_cmp.reshape(2, -1))
    w2p = jnp.concatenate([w_cmp2, jnp.zeros_like(w_cmp2)], axis=-1)
    w2n = jnp.zeros((2, N_KV, hid, N_KV * HEAD_DIM), F32)
    for k in range(N_KV):
        w2n = w2n.at[:, k, :, k * HEAD_DIM:(k + 1) * HEAD_DIM].set(w_cmp2)
    return dict(w1=w1.astype(BF16), w1f=w1f.astype(BF16), pe=pe.astype(BF16), w2p=w2p.astype(BF16),
                w2n=w2n.astype(BF16))


def _prep_kv_perm(w_kv):
    d = w_kv.shape[0]
    w6 = w_kv.reshape(d, 6, N_KV, HEAD_DIM)
    z = jnp.zeros((d, N_KV, HEAD_DIM), w_kv.dtype)
    ks = jnp.concatenate([w6[:, 2], z], axis=-1)
    kw = jnp.concatenate([w6[:, 4], z], axis=-1)
    vs = jnp.concatenate([w6[:, 3], z], axis=-1)
    vw = jnp.concatenate([z, w6[:, 5]], axis=-1)
    return jnp.concatenate([ks, kw, vs, vw], axis=1).reshape(d, 4 * N_KV * LANES).astype(BF16)


def _prep_attn_prompt(w_in, w_o):
    d = w_in.shape[0]
    qw = N_HEADS * HEAD_DIM
    scale = HEAD_DIM ** -0.5
    wq = (w_in[:, :qw] * scale).reshape(d, N_HEADS, HEAD_DIM)
    wq = jnp.concatenate([wq, jnp.zeros_like(wq)], axis=-1).reshape(d, N_HEADS * LANES)
    wg = w_in[:, qw:].reshape(d, N_HEADS, 3).transpose(0, 2, 1).reshape(d, 3 * N_HEADS)
    wg = jnp.concatenate([wg, jnp.zeros((d, LANES - 3 * N_HEADS), w_in.dtype)], axis=1)
    w_in_p = jnp.concatenate([wq, wg], axis=1).astype(BF16)
    return w_in_p, w_o.astype(BF16)


def _prep_attn_sample(w_in, w_o):
    d = w_in.shape[0]
    qw = N_HEADS * HEAD_DIM
    scale = HEAD_DIM ** -0.5
    wq = (w_in[:, :qw] * scale).reshape(d, N_KV, GROUP, HEAD_DIM).transpose(0, 2, 1, 3).reshape(d, qw)
    w_in_s = jnp.concatenate([wq, w_in[:, qw:]], axis=1).astype(BF16)
    wo_s = w_o.reshape(N_KV, GROUP, HEAD_DIM, w_o.shape[1]).transpose(1, 0, 2, 3).reshape(qw, w_o.shape[1])
    return w_in_s, wo_s.astype(BF16)


def kernel(x_prompt, x_sample, cache_kv, page_table, state_win, state_conv, c_prompt, c_sample, w_ada, b_ada, norm_g, conv_w_pw1, conv_b_pw1, conv_w_dw, conv_b_dw, conv_ln_g, conv_ln_b, conv_w_pw2, ffn_w_up, ffn_w_down, kv_norm_g, w_kv, w_cmp1, w_cmp2, pe_cmp, nsa_w_in, nsa_w_o, final_norm_g):
    b, t, d = x_prompt.shape
    bd, tn, _ = x_sample.shape
    depth = norm_g.shape[0]
    n_a = conv_w_pw1.shape[0]
    n_mod = b_ada.shape[0] // d
    past = page_table.shape[1] * PAGE
    hw = N_KV * HEAD_DIM
    assert bd % 8 == 0 and b <= 8 and n_mod == 6 * depth + 4

    c_all = jnp.concatenate([c_sample, c_prompt, jnp.zeros((8 - b % 8, d), F32)], axis=0)
    mods = _mods(c_all, w_ada, b_ada, n_mod)

    ffn_w = [_prep_ffn(ffn_w_up[l], ffn_w_down[l]) for l in range(depth)]
    w_pw1 = conv_w_pw1.astype(BF16)
    w_pw2 = conv_w_pw2.astype(BF16)
    w_kv_t = w_kv.T.astype(BF16)
    w_kv_perm = _prep_kv_perm(w_kv)
    cw = _prep_compress(w_cmp1, w_cmp2, pe_cmp)
    slopes = jnp.exp2(-8.0 * jnp.arange(1, N_HEADS + 1, dtype=F32) / N_HEADS)
    kblk_kv = (6 * depth) // 2
    kblk_final = (6 * depth + 2) // 2

    def trunk(x, time_major, conv_prev_tm, attn_fn_builder):
        conv_states = []
        attn_fn = None
        extra = None
        for l in range(depth):
            if l == n_a:
                attn_fn, extra = attn_fn_builder(x)
            if l < n_a:
                x, st = _conv_layer(x, None if conv_prev_tm is None else conv_prev_tm[l], mods, bd, l, time_major,
                                    norm_g[l, 0], w_pw1[l], conv_b_pw1[l], conv_w_dw[l], conv_b_dw[l],
                                    conv_ln_g[l], conv_ln_b[l], w_pw2[l])
                conv_states.append(st)
                o = wo = None
            else:
                o, wo = attn_fn(x, l)
            last = l == depth - 1
            x = _ffn_layer(x, mods, bd, l, time_major, norm_g[l, 1], ffn_w[l][0], ffn_w[l][1], o=o, wo=wo,
                           final_g=final_norm_g if last else None, n_mod_final=kblk_final)
        return x, extra, conv_states

    def from_feature_major(a, lead):
        nl = len(lead)
        a = a.reshape(lead + (-1, N_KV, HEAD_DIM, a.shape[-1]))
        return a.transpose(tuple(range(nl)) + (nl + 3, nl, nl + 1, nl + 2))

    attn_w_p = [_prep_attn_prompt(nsa_w_in[i], nsa_w_o[i]) for i in range(depth - n_a)]

    def prompt_attn_builder(x):
        kvt_pages, win_t, ks, kw, vs, vw = _kv_proj(x, mods, bd, kblk_kv, False, kv_norm_g, w_kv_t, w_kv_perm)
        ident = jnp.arange(b * t // PAGE, dtype=jnp.int32).reshape(b, t // PAGE)
        kcp, vcp, _, _ = _compress(kvt_pages, ident, cw)
        msel_t = _msel_t(t // STRIDE, t // L_SEL)

        def attn(xl, l):
            w_in_p, wo_p = attn_w_p[l - n_a]
            q, gates = _qin(xl, mods, bd, l, False, norm_g[l, 0], w_in_p)
            return _attn_prompt(slopes, q, gates, ks, kw, vs, vw, kcp, vcp, msel_t), wo_p

        extra = (from_feature_major(kvt_pages, (b * t // PAGE,)), from_feature_major(win_t, (b,)))
        return attn, extra

    y_prompt, (kv_prompt, win_prompt), conv_p = trunk(x_prompt, False, None, prompt_attn_builder)
    conv_prompt = jnp.stack([st[:, 32 - (CONV_W - 1):] for st in conv_p])

    attn_w_s = [_prep_attn_sample(nsa_w_in[i], nsa_w_o[i]) for i in range(depth - n_a)]
    cache_t = cache_kv.transpose(0, 2, 3, 4, 1).reshape(cache_kv.shape[0], 4 * hw, PAGE)
    nbuf = state_win.shape[1]
    state_win_t = state_win.transpose(0, 2, 3, 4, 1).reshape(bd, 2 * hw, nbuf)
    slope_rows = jnp.repeat(slopes, tn).reshape(N_HEADS * tn, 1)

    def sample_attn_builder(x):
        kvt = _kv_proj(x, mods, bd, kblk_kv, True, kv_norm_g, w_kv_t, None)
        kvt3 = kvt.reshape(6 * hw, tn, bd)
        kv_new = kvt3.transpose(2, 1, 0)
        _, _, kcn, vcn = _compress(cache_t, page_table, cw)
        n_s = -(-(past + tn) // L_SEL)
        msel_t = _msel_t(kcn.shape[1], -(-n_s // 8) * 8)

        def attn(xl, l):
            w_in_s, wo_s = attn_w_s[l - n_a]
            a = _qin(xl, mods, bd, l, True, norm_g[l, 0], w_in_s).transpose(1, 0, 2)
            q = a[..., :N_HEADS * HEAD_DIM]
            gates = a[..., N_HEADS * HEAD_DIM:].reshape(bd, tn, N_HEADS, 3).transpose(0, 3, 2, 1)
            gates = gates.reshape(bd, 3, N_HEADS * tn, 1)
            o = _attn_sample(page_table, cache_t, q, gates, kcn, vcn, state_win_t, kv_new, slope_rows, msel_t,
                             past)
            return o.transpose(1, 0, 2), wo_s

        n_keep = min(WINDOW, past + tn)
        win_all_t = jnp.concatenate([state_win_t, kvt3[4 * hw:].transpose(2, 0, 1)], axis=2)[:, :, -n_keep:]
        kv_sample = kvt3[:4 * hw].reshape(4, N_KV, HEAD_DIM, tn, bd).transpose(4, 3, 0, 1, 2)
        return attn, (kv_sample, from_feature_major(win_all_t, (bd,)))

    conv_prev_tm = state_conv.transpose(0, 2, 1, 3)
    y_s_tm, (kv_sample, win_sample), conv_s = trunk(x_sample.transpose(1, 0, 2), True, conv_prev_tm,
                                                    sample_attn_builder)
    y_sample = y_s_tm.transpose(1, 0, 2)
    conv_sample = jnp.stack(conv_s).transpose(0, 2, 1, 3)

    return (y_prompt, y_sample, kv_prompt, kv_sample, win_prompt, win_sample, conv_prompt, conv_sample)
```

```python
import functools

import jax
import jax.numpy as jnp
from jax import lax
from jax.experimental import pallas as pl
from jax.experimental.pallas import tpu as pltpu

F32 = jnp.float32
BF16 = jnp.bfloat16

N_HEADS = 16
N_KV = 4
GROUP = N_HEADS // N_KV
HEAD_DIM = 64
L_CMP = 32
STRIDE = 16
L_SEL = 64
TOP_N = 16
WINDOW = 512
Q_BLK = 128
PAGE = 128
CONV_W = 31
EPS = 1e-6
FORCE = 1e3
NEG = -1e30

LANES = 128
SUBLANES = 8
FF_CHUNK = 256
SEL_TILE = 256
PAGES_PER_STEP = 16
VMEM_LIMIT = 56 * 1024 * 1024


def _cparams(sem):
    return pltpu.CompilerParams(dimension_semantics=sem, vmem_limit_bytes=VMEM_LIMIT)


def _const_spec(shape):
    nd = len(shape)
    return pl.BlockSpec(shape, lambda *_: (0,) * nd, pipeline_mode=pl.Buffered(1))


def _dot(a, b):
    return jnp.dot(a, b, preferred_element_type=F32)


def _dot_nt(a, b):
    return lax.dot_general(a, b, (((1,), (1,)), ((), ())), preferred_element_type=F32)


def _silu(x):
    return x * jax.nn.sigmoid(x)


def _rms_mod(x, g, shift, scale):
    ms = jnp.mean(x * x, axis=-1, keepdims=True)
    y = x * lax.rsqrt(ms + EPS) * g
    return y * (1.0 + scale) + shift


def _mod_getter(m_ref, time_major):
    if time_major:
        return lambda k: m_ref[k][None]
    b = pl.program_id(0)
    return lambda k: m_ref[k, pl.ds(b, 1), :][None]


def _mods_kernel(c_ref, w_ref, b_ref, o_ref):
    c = c_ref[...]
    s = _silu(c).astype(BF16)
    o_ref[0] = _dot(s, w_ref[...].astype(BF16)) + b_ref[0]


def _mods(c_all, w_ada, b_ada, n_mod):
    r, d = c_all.shape
    return pl.pallas_call(
        _mods_kernel,
        grid=(n_mod,),
        in_specs=[pl.BlockSpec((r, d), lambda k: (0, 0)),
                  pl.BlockSpec((d, d), lambda k: (0, k)),
                  pl.BlockSpec((1, 1, d), lambda k: (k, 0, 0))],
        out_specs=pl.BlockSpec((1, r, d), lambda k: (k, 0, 0)),
        out_shape=jax.ShapeDtypeStruct((n_mod, r, d), F32),
        compiler_params=_cparams(("arbitrary",)),
        name="mods",
    )(c_all, w_ada, b_ada.reshape(n_mod, 1, d))


class _Tiling:
    def __init__(self, x_shape, time_major, mods_rows, bk=None, tr_max=512):
        self.time_major = time_major
        if time_major:
            tn, bd, d = x_shape
            bk = min(bd, 32) if bk is None else bk
            assert bd % bk == 0 and bk % 8 == 0
            self.grid = (bd // bk,)
            self.tile = (tn, bk, d)
            self.rows = tn * bk
            self.x_map = lambda j: (0, j, 0)
            self.mod_block = lambda k: (k, bk, d)
            self.mod_map = lambda kblk: (lambda j: (kblk, j, 0))
            self.sem = ("arbitrary",)
        else:
            b, t, d = x_shape
            tr = min(t, tr_max)
            assert t % tr == 0
            self.grid = (b, t // tr)
            self.tile = (1, tr, d)
            self.rows = tr
            self.x_map = lambda bi, i: (bi, i, 0)
            self.mod_block = lambda k: (k, 8, d)
            self.mod_map = lambda kblk: (lambda bi, i: (kblk, mods_rows // 8, 0))
            self.sem = ("arbitrary", "arbitrary")
        self.d = d

    def x_spec(self):
        return pl.BlockSpec(self.tile, self.x_map)

    def mod_spec(self, k, kblk):
        return pl.BlockSpec(self.mod_block(k), self.mod_map(kblk))


def _conv_core(y, ln_g, ln_b):
    mu = jnp.mean(y, axis=-1, keepdims=True)
    yc = y - mu
    var = jnp.mean(yc * yc, axis=-1, keepdims=True)
    return _silu(yc * lax.rsqrt(var + EPS) * ln_g + ln_b)


def _glu_rows(h2, w1_ref, b1_ref, d):
    a1 = _dot(h2, w1_ref[:, :d]) + b1_ref[:, :d]
    a2 = _dot(h2, w1_ref[:, d:]) + b1_ref[:, d:]
    return a1 * jax.nn.sigmoid(a2)


def _conv_prompt_kernel(x_ref, m_ref, g_ref, w1_ref, b1_ref, wdw_ref, bdw_ref, lng_ref, lnb_ref, w2_ref,
                        xo_ref, st_ref, s_ref, p_ref):
    tr, d = x_ref.shape[1], x_ref.shape[2]
    halo = 32
    off = halo - (CONV_W - 1)
    mod = _mod_getter(m_ref, False)

    @pl.when(pl.program_id(1) == 0)
    def _():
        s_ref[0:halo, :] = jnp.zeros((halo, d), F32)
        s_ref[halo + tr:halo + tr + SUBLANES, :] = jnp.zeros((SUBLANES, d), F32)

    x = x_ref[...]
    h = _rms_mod(x, g_ref[...][None], mod(0), mod(1))
    s_ref[halo:halo + tr, :] = _glu_rows(h.reshape(tr, d).astype(BF16), w1_ref, b1_ref, d)

    acc = jnp.broadcast_to(bdw_ref[...], (tr, d))
    for a in range(SUBLANES):
        part = None
        for r in range(a, off + CONV_W, SUBLANES):
            if r >= off:
                term = wdw_ref[r - off:r - off + 1, :] * s_ref[r - a:r - a + tr + SUBLANES, :]
                part = term if part is None else part + term
        if a == 0:
            acc = acc + part[0:tr]
        else:
            p_ref[...] = part
            acc = acc + p_ref[pl.ds(a, tr), :]
    z = _conv_core(acc, lng_ref[...], lnb_ref[...]).astype(BF16)
    y2 = _dot(z, w2_ref[...])
    xo_ref[...] = x + mod(2) * y2[None]
    tail = s_ref[tr:tr + halo, :]
    st_ref[0] = tail
    s_ref[0:halo, :] = tail


def _conv_sample_kernel(x_ref, p_ref, m_ref, g_ref, w1_ref, b1_ref, wdw_ref, bdw_ref, lng_ref, lnb_ref, w2_ref,
                        xo_ref, st_ref, s_ref):
    tn, bk, d = x_ref.shape
    npv = CONV_W - 1
    mod = _mod_getter(m_ref, True)
    x = x_ref[...]
    h = _rms_mod(x, g_ref[...][None], mod(0), mod(1))
    u = _glu_rows(h.reshape(tn * bk, d).astype(BF16), w1_ref, b1_ref, d).reshape(tn, bk, d)
    s_ref[0:npv] = p_ref[...]
    s_ref[npv:npv + tn] = u
    acc = jnp.broadcast_to(bdw_ref[...][None], (tn, bk, d))
    for w in range(CONV_W):
        acc = acc + wdw_ref[w:w + 1, :][None] * s_ref[w:w + tn]
    z = _conv_core(acc, lng_ref[...][None], lnb_ref[...][None]).astype(BF16)
    y2 = _dot(z.reshape(tn * bk, d), w2_ref[...]).reshape(tn, bk, d)
    xo_ref[...] = x + mod(2) * y2
    st_ref[...] = s_ref[tn:tn + npv]


def _conv_layer(x, prev_tm, mods, mods_rows, l, time_major, g1, w1, b1, wdw, bdw, lng, lnb, w2):
    tl = _Tiling(x.shape, time_major, mods_rows)
    d = tl.d
    weights = [g1.reshape(1, d), w1, b1.reshape(1, 2 * d), wdw, bdw.reshape(1, d), lng.reshape(1, d),
               lnb.reshape(1, d), w2]
    w_specs = [_const_spec(w.shape) for w in weights]
    if time_major:
        tn, bd, _ = x.shape
        bk = tl.tile[1]
        npv = CONV_W - 1
        return pl.pallas_call(
            _conv_sample_kernel,
            grid=tl.grid,
            in_specs=[tl.x_spec(), pl.BlockSpec((npv, bk, d), lambda j: (0, j, 0)), tl.mod_spec(6, l)] + w_specs,
            out_specs=[tl.x_spec(), pl.BlockSpec((npv, bk, d), lambda j: (0, j, 0))],
            out_shape=[jax.ShapeDtypeStruct(x.shape, F32), jax.ShapeDtypeStruct((npv, bd, d), F32)],
            scratch_shapes=[pltpu.VMEM((npv + tn, bk, d), F32)],
            compiler_params=_cparams(tl.sem),
            name="conv_sample",
        )(x, prev_tm, mods, *weights)
    b, t, _ = x.shape
    tr = tl.rows
    return pl.pallas_call(
        _conv_prompt_kernel,
        grid=tl.grid,
        in_specs=[tl.x_spec(), tl.mod_spec(6, l)] + w_specs,
        out_specs=[tl.x_spec(), pl.BlockSpec((1, 32, d), lambda bi, i: (bi, 0, 0))],
        out_shape=[jax.ShapeDtypeStruct(x.shape, F32), jax.ShapeDtypeStruct((b, 32, d), F32)],
        scratch_shapes=[pltpu.VMEM((tr + 32 + SUBLANES, d), F32), pltpu.VMEM((tr + SUBLANES, d), F32)],
        compiler_params=_cparams(tl.sem),
        name="conv_prompt",
    )(x, mods, *weights)


def _ffn_kernel(*refs, time_major, has_attn, final, head_major_o):
    it = iter(refs)
    x_ref = next(it)
    m_ref = next(it)
    g2_ref = next(it)
    if has_attn:
        o_ref = next(it)
        wo_ref = next(it)
    wup_ref = next(it)
    wdn_ref = next(it)
    if final:
        fm_ref = next(it)
        fg_ref = next(it)
    out_ref = next(it)
    hb_ref = next(it)
    acc_ref = next(it)

    a, bk, d = x_ref.shape
    rows = a * bk
    mod = _mod_getter(m_ref, time_major)
    x = x_ref[...]
    if has_attn:
        if head_major_o:
            ocat = jnp.concatenate([o_ref[0, h] for h in range(o_ref.shape[1])], axis=1)
        else:
            ocat = o_ref[...].reshape(rows, o_ref.shape[-1]).astype(BF16)
        x = x + mod(2) * _dot(ocat, wo_ref[...]).reshape(a, bk, d)
    h = _rms_mod(x, g2_ref[...][None], mod(3), mod(4))
    hb_ref[...] = h.reshape(rows, d).astype(BF16)
    acc_ref[...] = jnp.zeros((rows, d), F32)
    c = wdn_ref.shape[1]

    def body(j, carry):
        t = _dot(hb_ref[...], wup_ref[j])
        act = (_silu(t[:, :c]) * t[:, c:]).astype(BF16)
        acc_ref[...] += _dot(act, wdn_ref[j])
        return carry

    lax.fori_loop(0, wup_ref.shape[0], body, 0)
    x = x + mod(5) * acc_ref[...].reshape(a, bk, d)
    if final:
        fmod = _mod_getter(fm_ref, time_major)
        x = _rms_mod(x, fg_ref[...][None], fmod(0), fmod(1))
    out_ref[...] = x


def _ffn_layer(x, mods, mods_rows, l, time_major, g2, wup_c, wdn_c, o=None, wo=None, final_g=None, n_mod_final=None):
    has_attn = o is not None
    final = final_g is not None
    tl = _Tiling(x.shape, time_major, mods_rows, bk=x.shape[1] if time_major else None, tr_max=1024)
    d = tl.d
    args = [x, mods, g2.reshape(1, d)]
    specs = [tl.x_spec(), tl.mod_spec(6, l), _const_spec((1, d))]
    head_major_o = False
    if has_attn:
        if time_major:
            specs.append(pl.BlockSpec(tl.tile[:2] + (o.shape[-1],), tl.x_map))
        else:
            head_major_o = True
            tr = tl.rows
            specs.append(pl.BlockSpec((1, o.shape[1], tr, LANES), lambda bi, i: (bi, 0, i, 0)))
        args += [o, wo]
        specs.append(_const_spec(wo.shape))
    args += [wup_c, wdn_c]
    specs += [_const_spec(wup_c.shape), _const_spec(wdn_c.shape)]
    if final:
        args += [mods, final_g.reshape(1, d)]
        specs += [tl.mod_spec(2, n_mod_final), _const_spec((1, d))]
    kern = functools.partial(_ffn_kernel, time_major=time_major, has_attn=has_attn, final=final,
                             head_major_o=head_major_o)
    return pl.pallas_call(
        kern,
        grid=tl.grid,
        in_specs=specs,
        out_specs=tl.x_spec(),
        out_shape=jax.ShapeDtypeStruct(x.shape, F32),
        scratch_shapes=[pltpu.VMEM((tl.rows, d), BF16), pltpu.VMEM((tl.rows, d), F32)],
        compiler_params=_cparams(tl.sem),
        name="ffn_sample" if time_major else "ffn_prompt",
    )(*args)


def _kv_prompt_kernel(x_ref, m_ref, g_ref, wt_ref, wp_ref, kvt_ref, wint_ref, ks_ref, kw_ref, vs_ref, vw_ref):
    tr, d = x_ref.shape[1], x_ref.shape[2]
    mod = _mod_getter(m_ref, False)
    hk = _rms_mod(x_ref[...], g_ref[...][None], mod(0), mod(1)).reshape(tr, d).astype(BF16)
    kvt = _dot_nt(wt_ref[...], hk)
    n4 = kvt_ref.shape[1]
    for p in range(tr // PAGE):
        kvt_ref[p] = kvt[:n4, p * PAGE:(p + 1) * PAGE]
    wint_ref[0] = kvt[n4:, :]
    att = _dot(hk, wp_ref[...])
    t0 = pl.program_id(1) * tr
    blk = (t0 + lax.broadcasted_iota(jnp.int32, (tr, LANES), 0)) // L_SEL
    lane = lax.broadcasted_iota(jnp.int32, (tr, LANES), 1)
    onehot = lane - HEAD_DIM == blk
    low = lane < HEAD_DIM
    for k in range(N_KV):
        def part(j):
            return att[:, (j * N_KV + k) * LANES:(j * N_KV + k + 1) * LANES]
        ks_ref[0, k] = jnp.where(onehot, 1.0, part(0)).astype(BF16)
        kw_ref[0, k] = part(1).astype(BF16)
        vs_ref[0, k] = jnp.where(low, part(2), 1.0).astype(BF16)
        vw_ref[0, k] = jnp.where(low, 1.0, part(3)).astype(BF16)


def _kv_sample_kernel(x_ref, m_ref, g_ref, wt_ref, kvt_ref):
    tn, bk, d = x_ref.shape
    mod = _mod_getter(m_ref, True)
    hk = _rms_mod(x_ref[...], g_ref[...][None], mod(0), mod(1)).reshape(tn * bk, d).astype(BF16)
    kvt_ref[...] = _dot_nt(wt_ref[...], hk)


def _kv_proj(x, mods, mods_rows, kblk, time_major, g, w_t, w_perm):
    d = x.shape[-1]
    nkv = w_t.shape[0]
    if time_major:
        tn, bd, _ = x.shape
        tl = _Tiling(x.shape, True, mods_rows, bk=bd)
        return pl.pallas_call(
            _kv_sample_kernel,
            grid=tl.grid,
            in_specs=[tl.x_spec(), tl.mod_spec(2, kblk), _const_spec((1, d)), _const_spec(w_t.shape)],
            out_specs=pl.BlockSpec((nkv, tn * bd), lambda j: (0, 0)),
            out_shape=jax.ShapeDtypeStruct((nkv, tn * bd), F32),
            compiler_params=_cparams(tl.sem),
            name="kv_sample",
        )(x, mods, g.reshape(1, d), w_t)
    tl = _Tiling(x.shape, False, mods_rows)
    b, t, _ = x.shape
    tr = tl.rows
    n4 = 4 * N_KV * HEAD_DIM
    nw = min(WINDOW, t)
    assert tr == nw and tr % PAGE == 0
    ppt = tr // PAGE
    hm = pl.BlockSpec((1, N_KV, tr, LANES), lambda bi, i: (bi, 0, i, 0))
    hm_shape = jax.ShapeDtypeStruct((b, N_KV, t, LANES), BF16)
    return pl.pallas_call(
        _kv_prompt_kernel,
        grid=tl.grid,
        in_specs=[tl.x_spec(), tl.mod_spec(2, kblk), _const_spec((1, d)), _const_spec(w_t.shape),
                  _const_spec(w_perm.shape)],
        out_specs=[pl.BlockSpec((ppt, n4, PAGE), lambda bi, i: (bi * (t // tr) + i, 0, 0)),
                   pl.BlockSpec((1, nkv - n4, nw), lambda bi, i: (bi, 0, 0)), hm, hm, hm, hm],
        out_shape=[jax.ShapeDtypeStruct((b * t // PAGE, n4, PAGE), F32),
                   jax.ShapeDtypeStruct((b, nkv - n4, nw), F32), hm_shape, hm_shape, hm_shape, hm_shape],
        compiler_params=_cparams(tl.sem),
        name="kv_prompt",
    )(x, mods, g.reshape(1, d), w_t, w_perm)


def _compress_kernel(pt_ref, *refs, spb):
    del pt_ref
    n_pages = spb * PAGES_PER_STEP
    pages = refs[:n_pages]
    w1_ref, w1f_ref, pe_ref, w2p_ref, w2n_ref = refs[n_pages:n_pages + 5]
    kcp_ref, vcp_ref, kcn_ref, vcn_ref = refs[n_pages + 5:n_pages + 9]
    x_ref, hs_ref = refs[n_pages + 9:]
    nh = x_ref.shape[1] // spb
    rows_all = spb * nh
    hid = w2p_ref.shape[1]
    per_page = PAGE // STRIDE
    pg0 = pl.program_id(1) * PAGES_PER_STEP

    pi = lax.broadcasted_iota(jnp.int32, (PAGE, PAGE), 0)
    pj = lax.broadcasted_iota(jnp.int32, (PAGE, PAGE), 1)
    perm = jnp.where(pj == (pi % per_page) * STRIDE + pi // per_page, 1.0, 0.0).astype(BF16)
    for s in range(spb):
        for pg in range(PAGES_PER_STEP):
            r0 = pl.multiple_of(s * nh + (pg0 + pg) * per_page, per_page)
            rp = _dot_nt(perm, pages[s * PAGES_PER_STEP + pg][0].astype(BF16))
            for l in range(STRIDE):
                x_ref[l, pl.ds(r0, per_page), :] = rp[l * per_page:(l + 1) * per_page, :]

    @pl.when(pl.program_id(1) == pl.num_programs(1) - 1)
    def _():
        rows = lax.broadcasted_iota(jnp.int32, (rows_all, hid), 0) % nh
        hs_ref[rows_all:rows_all + 8, :] = jnp.zeros((8, hid), F32)
        for c, (pad_ref, nat_ref) in enumerate(((kcp_ref, kcn_ref), (vcp_ref, vcn_ref))):
            cvec = _dot(pe_ref[c], w1f_ref[c])[0:1, :]
            nat = jnp.zeros((rows_all, N_KV * HEAD_DIM), F32)
            for pr in range(N_KV // 2):
                lo = c * N_KV * HEAD_DIM + pr * LANES
                xcat = jnp.concatenate([x_ref[l, :, lo:lo + LANES] for l in range(STRIDE)], axis=1)
                acc = _dot(xcat.astype(BF16), w1_ref[c])
                for e in range(2):
                    k = 2 * pr + e
                    hs_ref[0:rows_all, :] = acc[:, (2 * e + 1) * hid:(2 * e + 2) * hid]
                    hsum = acc[:, 2 * e * hid:(2 * e + 1) * hid] + hs_ref[pl.ds(1, rows_all), :] + cvec
                    hb = jnp.where(rows < nh - 1, _silu(hsum), 0.0).astype(BF16)
                    padded = _dot(hb, w2p_ref[c]).astype(BF16)
                    for s in range(spb):
                        pad_ref[s, k] = padded[s * nh:(s + 1) * nh]
                    nat = nat + _dot(hb, w2n_ref[c, k])
            for s in range(spb):
                nat_ref[s] = nat[s * nh:(s + 1) * nh].astype(BF16)


def _compress(pages, page_table, cw):
    bs, p = page_table.shape
    assert p % PAGES_PER_STEP == 0
    ng = p // PAGES_PER_STEP
    nh = p * (PAGE // STRIDE)
    wcols = 2 * N_KV * HEAD_DIM
    hid = cw["w2p"].shape[1]
    spb = 2 if bs % 2 == 0 else 1

    def page_spec(s, pg):
        return pl.BlockSpec((1, wcols, PAGE), lambda b, g, pt: (pt[b * spb + s, g * PAGES_PER_STEP + pg], 0, 0))

    def full(a):
        nd = a.ndim
        return pl.BlockSpec(a.shape, lambda b, g, pt: (0,) * nd)

    consts = [cw["w1"], cw["w1f"], cw["pe"], cw["w2p"], cw["w2n"]]
    pad_shape = jax.ShapeDtypeStruct((bs, N_KV, nh, LANES), BF16)
    nat_shape = jax.ShapeDtypeStruct((bs, nh, N_KV * HEAD_DIM), BF16)
    pad_spec = pl.BlockSpec((spb, N_KV, nh, LANES), lambda b, g, pt: (b, 0, 0, 0))
    nat_spec = pl.BlockSpec((spb, nh, N_KV * HEAD_DIM), lambda b, g, pt: (b, 0, 0))
    grid_spec = pltpu.PrefetchScalarGridSpec(
        num_scalar_prefetch=1,
        grid=(bs // spb, ng),
        in_specs=[page_spec(s, pg) for s in range(spb) for pg in range(PAGES_PER_STEP)] + [full(a) for a in consts],
        out_specs=[pad_spec, pad_spec, nat_spec, nat_spec],
        scratch_shapes=[pltpu.VMEM((STRIDE, spb * nh, wcols), F32), pltpu.VMEM((spb * nh + 8, hid), F32)],
    )
    return pl.pallas_call(
        functools.partial(_compress_kernel, spb=spb),
        grid_spec=grid_spec,
        out_shape=[pad_shape, pad_shape, nat_shape, nat_shape],
        compiler_params=_cparams(("arbitrary", "arbitrary")),
        name="compress",
    )(page_table, *([pages] * (spb * PAGES_PER_STEP)), *consts)


def _qin_prompt_kernel(x_ref, m_ref, g_ref, w_ref, q_ref, gate_ref):
    tr, d = x_ref.shape[1], x_ref.shape[2]
    mod = _mod_getter(m_ref, False)
    h = _rms_mod(x_ref[...], g_ref[...][None], mod(0), mod(1)).reshape(tr, d).astype(BF16)
    a = _dot(h, w_ref[...])
    for hd in range(N_HEADS):
        q_ref[0, hd] = a[:, hd * LANES:(hd + 1) * LANES].astype(BF16)
    gate_ref[0] = a[:, N_HEADS * LANES:]


def _qin_sample_kernel(x_ref, m_ref, g_ref, w_ref, a_ref):
    tn, bk, d = x_ref.shape
    mod = _mod_getter(m_ref, True)
    h = _rms_mod(x_ref[...], g_ref[...][None], mod(0), mod(1)).reshape(tn * bk, d).astype(BF16)
    a_ref[...] = _dot(h, w_ref[...]).reshape(tn, bk, a_ref.shape[-1])


def _qin(x, mods, mods_rows, l, time_major, g1, w):
    tl = _Tiling(x.shape, time_major, mods_rows, bk=x.shape[1] if time_major else None)
    d = tl.d
    n = w.shape[1]
    common = dict(grid=tl.grid, compiler_params=_cparams(tl.sem))
    in_specs = [tl.x_spec(), tl.mod_spec(6, l), _const_spec((1, d)), _const_spec(w.shape)]
    if time_major:
        return pl.pallas_call(
            _qin_sample_kernel, in_specs=in_specs,
            out_specs=pl.BlockSpec(tl.tile[:2] + (n,), tl.x_map),
            out_shape=jax.ShapeDtypeStruct(x.shape[:2] + (n,), F32),
            name="qin_sample", **common)(x, mods, g1.reshape(1, d), w)
    b, t, _ = x.shape
    tr = tl.rows
    return pl.pallas_call(
        _qin_prompt_kernel, in_specs=in_specs,
        out_specs=[pl.BlockSpec((1, N_HEADS, tr, LANES), lambda bi, i: (bi, 0, i, 0)),
                   pl.BlockSpec((1, tr, LANES), tl.x_map)],
        out_shape=[jax.ShapeDtypeStruct((b, N_HEADS, t, LANES), BF16), jax.ShapeDtypeStruct((b, t, LANES), F32)],
        name="qin_prompt", **common)(x, mods, g1.reshape(1, d), w)


def _block_rank(score_ref, ns):
    score = score_ref[...]
    blk = lax.broadcasted_iota(jnp.int32, score.shape, 0)
    rank = jnp.zeros(score.shape, F32)
    for i in range(ns):
        si = score_ref[i:i + 1, :]
        rank = rank + jnp.where(blk > i, jnp.where(si >= score, 1.0, 0.0), jnp.where(si > score, 1.0, 0.0))
    return rank


def _add_alibi(s, slopes_ref, head0, pos_f, qb):
    return jnp.concatenate([s[g * qb:(g + 1) * qb] + slopes_ref[head0 + g] * pos_f for g in range(GROUP)], axis=0)


def _attn_prompt_kernel(slopes_ref, q_ref, gate_ref, ks_ref, kw_ref, vs_ref, vw_ref, kc_ref, vc_ref, msel_ref,
                        ega_ref, egb_ref, o_ref,
                        s_ref, sc_ref, mt_ref, qa_ref, oc_ref, as_ref, mx_ref, ga_ref, gb_ref, ids_ref, cnt_ref):
    qb = q_ref.shape[2]
    rows = GROUP * qb
    t = ks_ref.shape[2]
    nc = kc_ref.shape[2]
    ns = msel_ref.shape[0]
    max_tiles = t // SEL_TILE
    blocks_per_tile = SEL_TILE // L_SEL
    i = pl.program_id(1)
    s0 = i * qb
    n_tiles = (s0 + qb + SEL_TILE - 1) // SEL_TILE
    qpos4 = s0 + lax.broadcasted_iota(jnp.int32, (rows, 1), 0) % qb
    low4 = lax.broadcasted_iota(jnp.int32, (rows, LANES), 1) < HEAD_DIM
    low1 = lax.broadcasted_iota(jnp.int32, (qb, LANES), 1) < HEAD_DIM

    gs = jax.nn.sigmoid(gate_ref[0])
    g_hi = gs.astype(BF16)
    g_lo = (gs - g_hi.astype(F32)).astype(BF16)
    g2 = jnp.concatenate([g_hi, g_lo], axis=1)
    ga = _dot(g2, ega_ref[...])
    gb = _dot(g2, egb_ref[...])
    for h in range(N_HEADS):
        ga_ref[h] = ga[:, h * LANES:(h + 1) * LANES]
        gb_ref[h] = gb[:, h * LANES:(h + 1) * LANES]

    cidx = lax.broadcasted_iota(jnp.int32, (1, nc), 1)
    cend = cidx * STRIDE + (L_CMP - 1)
    cmask = (cend <= qpos4) & (cidx < nc - 1)
    cend_f = cend.astype(F32)
    row_ok = qpos4 >= L_CMP - 1
    blk = lax.broadcasted_iota(jnp.int32, (ns, qb), 0)
    qpos_t = s0 + lax.broadcasted_iota(jnp.int32, (ns, qb), 1)
    cur = qpos_t // L_SEL
    valid_t = blk * L_SEL <= qpos_t
    forced = (blk == 0) | (blk == cur) | (blk == cur - 1)
    tile_of_blk = jnp.where(lax.broadcasted_iota(jnp.int32, (max_tiles, ns), 1) // blocks_per_tile
                            == lax.broadcasted_iota(jnp.int32, (max_tiles, ns), 0), 1.0, 0.0).astype(BF16)

    for kvh in range(N_KV):
        q4 = q_ref[0, kvh * GROUP:(kvh + 1) * GROUP].reshape(rows, LANES)
        sc = _add_alibi(_dot_nt(q4, kc_ref[0, kvh]), slopes_ref, kvh * GROUP, cend_f, qb)
        sc = jnp.where(cmask, sc, NEG)
        e = jnp.exp(sc - jnp.max(sc, axis=1, keepdims=True))
        r = jnp.where(row_ok, 1.0 / jnp.sum(e, axis=1, keepdims=True), 0.0)
        pb = (e * r).astype(BF16)
        oc_ref[kvh] = _dot(pb, vc_ref[0, kvh])
        imp4 = _dot_nt(msel_ref[...], pb)
        imp_t = imp4[:, 0:qb]
        for g in range(1, GROUP):
            imp_t = imp_t + imp4[:, g * qb:(g + 1) * qb]
        sc_ref[kvh] = jnp.where(valid_t, imp_t + jnp.where(forced, FORCE, 0.0), -jnp.inf)
        keep = (_block_rank(sc_ref.at[kvh], ns) < TOP_N) & valid_t
        mt_ref[kvh] = jnp.zeros(mt_ref.shape[1:], F32)
        mt_ref[kvh, HEAD_DIM:HEAD_DIM + ns, :] = jnp.where(keep, 0.0, NEG)
        mask_b = mt_ref[kvh].T.astype(BF16)
        qa_ref[kvh] = jnp.where(low4, q4, jnp.concatenate([mask_b] * GROUP, axis=0))

        per_tile = jnp.sum(_dot(tile_of_blk, jnp.where(keep, 1.0, 0.0).astype(BF16)), axis=1, keepdims=True)
        for pos in range(ids_ref.shape[1]):
            ids_ref[kvh, pos] = jnp.int32(-1)
        n_act = jnp.int32(0)
        for kt in range(max_tiles - 1):
            active = jnp.logical_and(kt < n_tiles - 1, per_tile[kt, 0] > 0.5)
            ids_ref[kvh, n_act] = jnp.where(active, jnp.int32(kt), jnp.int32(-1))
            n_act = n_act + active.astype(jnp.int32)
        ids_ref[kvh, n_act] = n_tiles - 1
        cnt_ref[kvh] = n_act + 1

    lane_blocks = SEL_TILE // LANES
    for pair in range(N_KV // 2):
        kvs = (2 * pair, 2 * pair + 1)
        n_steps = (jnp.maximum(cnt_ref[kvs[0]], cnt_ref[kvs[1]]) + 1) // 2
        for kvh in kvs:
            mx_ref[kvh] = jnp.full((rows, LANES), NEG, F32)
            as_ref[kvh] = jnp.zeros((rows, LANES), F32)

        def entries(step):
            out = []
            for u in range(2):
                for j, kvh in enumerate(kvs):
                    kt = ids_ref[kvh, 2 * step + u]
                    null = kt < 0
                    out.append((j, kvh, jnp.maximum(kt, 0), jnp.where(null, max_tiles, kt), null))
            return out

        def score_step(step, carry):
            for j, kvh, kt, slot, null in entries(step):
                k0 = pl.multiple_of(kt * SEL_TILE, SEL_TILE)
                kpos = k0 + lax.broadcasted_iota(jnp.int32, (1, SEL_TILE), 1)
                s = _dot_nt(qa_ref[kvh], ks_ref[0, kvh, pl.ds(k0, SEL_TILE), :])
                s = _add_alibi(s, slopes_ref, kvh * GROUP, kpos.astype(F32), qb)
                s = jnp.where(kpos + jnp.where(null, jnp.int32(1 << 30), 0) <= qpos4, s, NEG)
                s_ref[j, slot] = s
                m = s[:, 0:LANES]
                for c in range(1, lane_blocks):
                    m = jnp.maximum(m, s[:, c * LANES:(c + 1) * LANES])
                mx_ref[kvh] = jnp.maximum(mx_ref[kvh], m)
            return carry

        lax.fori_loop(0, n_steps, score_step, 0)
        mrow = [jnp.max(mx_ref[kvh], axis=1, keepdims=True) for kvh in kvs]

        def weight_step(step, carry):
            for j, kvh, kt, slot, null in entries(step):
                k0 = pl.multiple_of(kt * SEL_TILE, SEL_TILE)
                p = jnp.exp(s_ref[j, slot] - mrow[j]).astype(BF16)
                as_ref[kvh] += _dot(p, vs_ref[0, kvh, pl.ds(k0, SEL_TILE), :])
            return carry

        lax.fori_loop(0, n_steps, weight_step, 0)

    wlen = WINDOW + qb
    w0 = pl.multiple_of(jnp.maximum(s0 - WINDOW, 0), qb)
    wpos = w0 + lax.broadcasted_iota(jnp.int32, (1, wlen), 1)
    wpos_f = wpos.astype(F32)
    dist = qpos4 - wpos
    wmask = (dist >= 0) & (dist < WINDOW)
    for kvh in range(N_KV):
        q4 = q_ref[0, kvh * GROUP:(kvh + 1) * GROUP].reshape(rows, LANES)
        s = _add_alibi(_dot_nt(q4, kw_ref[0, kvh, pl.ds(w0, wlen), :]), slopes_ref, kvh * GROUP, wpos_f, qb)
        s = jnp.where(wmask, s, NEG)
        p = jnp.exp(s - jnp.max(s, axis=1, keepdims=True)).astype(BF16)
        acc_w = _dot(p, vw_ref[0, kvh, pl.ds(w0, wlen), :])
        acc_s = as_ref[kvh]
        o_s = acc_s * (1.0 / pltpu.roll(acc_s, HEAD_DIM, 1))
        o_w = acc_w * (1.0 / pltpu.roll(acc_w, HEAD_DIM, 1))
        ga4 = ga_ref[kvh * GROUP:(kvh + 1) * GROUP].reshape(rows, LANES)
        gb4 = gb_ref[kvh * GROUP:(kvh + 1) * GROUP].reshape(rows, LANES)
        out = ga4 * jnp.where(low4, oc_ref[kvh], o_w) + gb4 * jnp.where(low4, o_s, 0.0)
        out = out + pltpu.roll(out, HEAD_DIM, 1)
        for pr in range(GROUP // 2):
            even = out[(2 * pr) * qb:(2 * pr + 1) * qb]
            odd = out[(2 * pr + 1) * qb:(2 * pr + 2) * qb]
            o_ref[0, kvh * (GROUP // 2) + pr] = jnp.where(low1, even, odd).astype(BF16)


def _gate_expand():
    src = jnp.arange(2 * LANES) % LANES
    br = (src // N_HEADS)[:, None]
    hd = (src % N_HEADS)[:, None]
    col = jnp.arange(N_HEADS * LANES)
    same = hd == (col // LANES)[None, :]
    low = (col % LANES < HEAD_DIM)[None, :]
    ea = same & (((br == 0) & low) | ((br == 2) & ~low))
    eb = same & (br == 1) & low
    return ea.astype(BF16), eb.astype(BF16)


def _attn_prompt(slopes, q, gates, ks, kw, vs, vw, kcp, vcp, msel_t):
    b, _, t, _ = q.shape
    nc = kcp.shape[2]
    ns = msel_t.shape[0]
    assert t % SEL_TILE == 0 and t >= WINDOW + Q_BLK and ns <= HEAD_DIM and ns % 8 == 0
    ea, eb = _gate_expand()
    rows = GROUP * Q_BLK
    max_tiles = t // SEL_TILE
    qspec = pl.BlockSpec((1, N_HEADS, Q_BLK, LANES), lambda bi, i: (bi, 0, i, 0))
    kvspec = pl.BlockSpec((1, N_KV, t, LANES), lambda bi, i: (bi, 0, 0, 0), pipeline_mode=pl.Buffered(1))
    cspec = pl.BlockSpec((1, N_KV, nc, LANES), lambda bi, i: (bi, 0, 0, 0))
    return pl.pallas_call(
        _attn_prompt_kernel,
        grid=(b, t // Q_BLK),
        in_specs=[pl.BlockSpec(memory_space=pltpu.SMEM), qspec,
                  pl.BlockSpec((1, Q_BLK, LANES), lambda bi, i: (bi, i, 0)),
                  kvspec, kvspec, kvspec, kvspec, cspec, cspec,
                  _const_spec(msel_t.shape), _const_spec(ea.shape), _const_spec(eb.shape)],
        out_specs=pl.BlockSpec((1, N_HEADS // 2, Q_BLK, LANES), lambda bi, i: (bi, 0, i, 0)),
        out_shape=jax.ShapeDtypeStruct((b, N_HEADS // 2, t, LANES), BF16),
        scratch_shapes=[pltpu.VMEM((2, max_tiles + 1, rows, SEL_TILE), F32),
                        pltpu.VMEM((N_KV, ns, Q_BLK), F32), pltpu.VMEM((N_KV, LANES, Q_BLK), F32),
                        pltpu.VMEM((N_KV, rows, LANES), BF16), pltpu.VMEM((N_KV, rows, LANES), F32),
                        pltpu.VMEM((N_KV, rows, LANES), F32), pltpu.VMEM((N_KV, rows, LANES), F32),
                        pltpu.VMEM((N_HEADS, Q_BLK, LANES), F32), pltpu.VMEM((N_HEADS, Q_BLK, LANES), F32),
                        pltpu.SMEM((N_KV, max_tiles + 2), jnp.int32), pltpu.SMEM((N_KV,), jnp.int32)],
        compiler_params=_cparams(("arbitrary", "arbitrary")),
        name="attn_prompt",
    )(slopes, q, gates, ks, kw, vs, vw, kcp, vcp, msel_t, ea, eb)


def _attn_sample_kernel(pt_ref, *refs, past, spb):
    del pt_ref
    npg = past // PAGE
    for seq in range(spb):
        _attn_sample_seq(seq, refs[seq * npg:(seq + 1) * npg], *refs[spb * npg:], past=past)


def _attn_sample_seq(seq, pages, q_ref, gate_ref, kc_ref, vc_ref, sw_ref, kvn_ref, slope_ref, msel_ref, o_ref,
                     s_ref, sc_ref, st_ref, *, past):
    npg = past // PAGE
    s_ref, sc_ref, st_ref = s_ref.at[seq], sc_ref.at[seq], st_ref.at[seq]
    tn = q_ref.shape[1]
    hw = N_KV * HEAD_DIM
    rows = N_KV * GROUP * tn
    nc = kc_ref.shape[1]
    ns_pad = sc_ref.shape[0]
    nblk = npg + 1
    n_s = (past + tn + L_SEL - 1) // L_SEL
    f_sel = 2 * hw
    slope = slope_ref[...]

    rr = lax.broadcasted_iota(jnp.int32, (rows, 1), 0)
    qpos = past + rr % tn
    col = lax.broadcasted_iota(jnp.int32, (tn, hw), 1) // HEAD_DIM
    colr = lax.broadcasted_iota(jnp.int32, (rows, hw), 1) // HEAD_DIM
    rowk = lax.broadcasted_iota(jnp.int32, (rows, hw), 0) // (GROUP * tn)
    diag = colr == rowk

    qs = q_ref[seq]
    pieces = []
    for k in range(N_KV):
        for g in range(GROUP):
            pieces.append(jnp.where(col == k, qs[:, g * hw:(g + 1) * hw], 0.0))
    qbd = jnp.concatenate(pieces, axis=0).astype(BF16)

    cidx = lax.broadcasted_iota(jnp.int32, (1, nc), 1)
    cend = cidx * STRIDE + (L_CMP - 1)
    cvalid = (cend <= qpos) & (cidx < nc - 1)
    sc = jnp.where(cvalid, _dot_nt(qbd, kc_ref[seq]) + slope * cend.astype(F32), NEG)
    e = jnp.exp(sc - jnp.max(sc, axis=1, keepdims=True))
    r = jnp.where(qpos >= L_CMP - 1, 1.0 / jnp.sum(e, axis=1, keepdims=True), 0.0)
    pc = (e * r).astype(BF16)
    o_c = _dot(pc, vc_ref[seq])

    imp = _dot_nt(msel_ref[...], pc)
    tot = imp
    for g in range(1, GROUP):
        tot = tot + pltpu.roll(imp, rows - g * tn, 1)
    blk = lax.broadcasted_iota(jnp.int32, (ns_pad, rows), 0)
    lane_r = lax.broadcasted_iota(jnp.int32, (ns_pad, rows), 1)
    qpos_l = past + lane_r % tn
    cur = qpos_l // L_SEL
    valid_l = (blk * L_SEL <= qpos_l) & (blk < n_s)
    forced = (blk == 0) | (blk == cur) | (blk == cur - 1)
    sc_ref[...] = jnp.where(valid_l, tot + jnp.where(forced, FORCE, 0.0), -jnp.inf)
    keep = (_block_rank(sc_ref, n_s) < TOP_N) & valid_l
    lead = (lane_r // tn) % GROUP == 0
    keep_f = jnp.where(keep & lead, 1.0, 0.0)
    spread = keep_f
    for g in range(1, GROUP):
        spread = spread + pltpu.roll(keep_f, g * tn, 1)
    st_ref[...] = jnp.zeros(st_ref.shape, F32)
    st_ref[0:ns_pad, :] = spread
    keep_rows = st_ref[...].T.astype(BF16)

    zpad = jnp.zeros((PAGE - tn, hw), F32)
    bsel = lax.broadcasted_iota(jnp.int32, (LANES, PAGE), 0)
    ksub = lax.broadcasted_iota(jnp.int32, (LANES, PAGE), 1) // L_SEL
    kloc = lax.broadcasted_iota(jnp.int32, (1, PAGE), 1)
    mrun = jnp.full((rows, 1), NEG, F32)
    for j in range(nblk):
        if j < npg:
            qk = _dot(qbd, pages[j][0, 0:hw, :].astype(BF16))
        else:
            kb = jnp.concatenate([kvn_ref[seq, :,f_sel:f_sel + hw], zpad], axis=0).astype(BF16)
            qk = _dot_nt(qbd, kb)
        kpos = j * PAGE + kloc
        expand = jnp.where(bsel == (PAGE // L_SEL) * j + ksub, 1.0, 0.0).astype(BF16)
        kept = _dot(keep_rows, expand)
        s = jnp.where((kept > 0.5) & (kpos <= qpos), qk + slope * kpos.astype(F32), NEG)
        s_ref[:, j * PAGE:(j + 1) * PAGE] = s
        mrun = jnp.maximum(mrun, jnp.max(s, axis=1, keepdims=True))
    lsum = jnp.zeros((rows, 1), F32)
    o_s = jnp.zeros((rows, hw), F32)
    for j in range(nblk):
        p = jnp.exp(s_ref[:, j * PAGE:(j + 1) * PAGE] - mrun)
        lsum = lsum + jnp.sum(p, axis=1, keepdims=True)
        if j < npg:
            o_s = o_s + _dot_nt(p.astype(BF16), pages[j][0, hw:2 * hw, :].astype(BF16))
        else:
            vb = jnp.concatenate([kvn_ref[seq, :,f_sel + hw:f_sel + 2 * hw], zpad], axis=0).astype(BF16)
            o_s = o_s + _dot(p.astype(BF16), vb)
    o_s = o_s * (1.0 / lsum)

    nbuf = sw_ref.shape[2]
    f_win = 4 * hw
    kwn = jnp.concatenate([kvn_ref[seq, :,f_win:f_win + hw], zpad], axis=0).astype(BF16)
    vwn = jnp.concatenate([kvn_ref[seq, :,f_win + hw:f_win + 2 * hw], zpad], axis=0).astype(BF16)
    pos_a = past - nbuf + lax.broadcasted_iota(jnp.int32, (1, nbuf), 1)
    pos_b = past + kloc
    da = qpos - pos_a
    db = qpos - pos_b
    va = (da >= 0) & (da < WINDOW) & (pos_a >= 0)
    vb_ok = (db >= 0) & (db < WINDOW)
    s_a = jnp.where(va, _dot(qbd, sw_ref[seq,0:hw, :].astype(BF16)) + slope * pos_a.astype(F32), NEG)
    s_b = jnp.where(vb_ok, _dot_nt(qbd, kwn) + slope * pos_b.astype(F32), NEG)
    m = jnp.maximum(jnp.max(s_a, axis=1, keepdims=True), jnp.max(s_b, axis=1, keepdims=True))
    p_a = jnp.exp(s_a - m)
    p_b = jnp.exp(s_b - m)
    l_w = jnp.sum(p_a, axis=1, keepdims=True) + jnp.sum(p_b, axis=1, keepdims=True)
    o_w = (_dot_nt(p_a.astype(BF16), sw_ref[seq,hw:2 * hw, :].astype(BF16)) + _dot(p_b.astype(BF16), vwn)) * (1.0 / l_w)

    gt = jax.nn.sigmoid(gate_ref[seq])
    tot_o = jnp.where(diag, gt[0] * o_c + gt[1] * o_s + gt[2] * o_w, 0.0)
    for g in range(GROUP):
        acc = jnp.zeros((tn, hw), F32)
        for k in range(N_KV):
            r0 = (k * GROUP + g) * tn
            acc = acc + tot_o[r0:r0 + tn, :]
        o_ref[seq, :, g * hw:(g + 1) * hw] = acc


def _attn_sample(page_table, cache_t, q, gates, kcn, vcn, state_win_t, kv_new, slope_rows, msel_t, past):
    bd, tn, qw = q.shape
    npg = past // PAGE
    hw = N_KV * HEAD_DIM
    rows = N_KV * GROUP * tn
    assert rows == LANES and msel_t.shape[1] == kcn.shape[1]
    ns_pad = msel_t.shape[0]
    spb = 2 if bd % 2 == 0 else 1

    def page_spec(s, pg):
        return pl.BlockSpec((1, 2 * hw, PAGE), lambda b, pt: (pt[b * spb + s, pg], 1, 0))

    def per_seq(a):
        nd = a.ndim
        return pl.BlockSpec((spb,) + a.shape[1:], lambda b, pt: (b,) + (0,) * (nd - 1))

    def full(a):
        nd = a.ndim
        return pl.BlockSpec(a.shape, lambda b, pt: (0,) * nd)

    grid_spec = pltpu.PrefetchScalarGridSpec(
        num_scalar_prefetch=1,
        grid=(bd // spb,),
        in_specs=[page_spec(s, pg) for s in range(spb) for pg in range(npg)]
        + [per_seq(q), per_seq(gates), per_seq(kcn), per_seq(vcn), per_seq(state_win_t), per_seq(kv_new),
           full(slope_rows), full(msel_t)],
        out_specs=pl.BlockSpec((spb, tn, N_HEADS * HEAD_DIM), lambda b, pt: (b, 0, 0)),
        scratch_shapes=[pltpu.VMEM((spb, rows, (npg + 1) * PAGE), F32), pltpu.VMEM((spb, ns_pad, rows), F32),
                        pltpu.VMEM((spb, LANES, rows), F32)],
    )
    return pl.pallas_call(
        functools.partial(_attn_sample_kernel, past=past, spb=spb),
        grid_spec=grid_spec,
        out_shape=jax.ShapeDtypeStruct((bd, tn, N_HEADS * HEAD_DIM), F32),
        compiler_params=_cparams(("arbitrary",)),
        name="attn_sample",
    )(page_table, *([cache_t] * (spb * npg)), q, gates, kcn, vcn, state_win_t, kv_new, slope_rows, msel_t)


def _msel_t(n_c_rows, n_s_rows):
    i0 = jnp.arange(n_c_rows)[None, :] * STRIDE
    j0 = jnp.arange(n_s_rows)[:, None] * L_SEL
    return ((i0 < j0 + L_SEL) & (i0 + L_CMP > j0)).astype(BF16)


def _prep_ffn(w_up, w_down):
    d, two_ff = w_up.shape
    ff = two_ff // 2
    n = ff // FF_CHUNK
    assert n * FF_CHUNK == ff
    a = w_up[:, :ff].reshape(d, n, FF_CHUNK)
    b = w_up[:, ff:].reshape(d, n, FF_CHUNK)
    wup_c = jnp.concatenate([a, b], axis=2).transpose(1, 0, 2).astype(BF16)
    wdn_c = w_down.reshape(n, FF_CHUNK, w_down.shape[1]).astype(BF16)
    return wup_c, wdn_c


def _prep_compress(w_cmp1, w_cmp2, pe_cmp):
    hid = w_cmp1.shape[-1]
    wl = jnp.concatenate([w_cmp1[:, :STRIDE], w_cmp1[:, STRIDE:]], axis=-1)
    z = jnp.zeros_like(wl)
    w1 = jnp.concatenate([jnp.concatenate([wl, z], axis=-1), jnp.concatenate([z, wl], axis=-1)], axis=2)
    w1 = w1.reshape(2, STRIDE * LANES, 4 * hid)
    w1f = w_cmp1.reshape(2, L_CMP * HEAD_DIM, hid)
    pe = jnp.zeros((2, 8, L_CMP * HEAD_DIM), F32).at[:, 0].set(pe_cmp.reshape(2, -1))
    w2p = jnp.concatenate([w_cmp2, jnp.zeros_like(w_cmp2)], axis=-1)
    w2n = jnp.zeros((2, N_KV, hid, N_KV * HEAD_DIM), F32)
    for k in range(N_KV):
        w2n = w2n.at[:, k, :, k * HEAD_DIM:(k + 1) * HEAD_DIM].set(w_cmp2)
    return dict(w1=w1.astype(BF16), w1f=w1f.astype(BF16), pe=pe.astype(BF16), w2p=w2p.astype(BF16),
                w2n=w2n.astype(BF16))


def _prep_kv_perm(w_kv):
    d = w_kv.shape[0]
    w6 = w_kv.reshape(d, 6, N_KV, HEAD_DIM)
    z = jnp.zeros((d, N_KV, HEAD_DIM), w_kv.dtype)
    ks = jnp.concatenate([w6[:, 2], z], axis=-1)
    kw = jnp.concatenate([w6[:, 4], z], axis=-1)
    vs = jnp.concatenate([w6[:, 3], z], axis=-1)
    vw = jnp.concatenate([z, w6[:, 5]], axis=-1)
    return jnp.concatenate([ks, kw, vs, vw], axis=1).reshape(d, 4 * N_KV * LANES).astype(BF16)


def _prep_attn_prompt(w_in, w_o):
    d = w_in.shape[0]
    qw = N_HEADS * HEAD_DIM
    scale = HEAD_DIM ** -0.5
    wq = (w_in[:, :qw] * scale).reshape(d, N_HEADS, HEAD_DIM)
    wq = jnp.concatenate([wq, jnp.zeros_like(wq)], axis=-1).reshape(d, N_HEADS * LANES)
    wg = w_in[:, qw:].reshape(d, N_HEADS, 3).transpose(0, 2, 1).reshape(d, 3 * N_HEADS)
    wg = jnp.concatenate([wg, jnp.zeros((d, LANES - 3 * N_HEADS), w_in.dtype)], axis=1)
    w_in_p = jnp.concatenate([wq, wg], axis=1).astype(BF16)
    return w_in_p, w_o.astype(BF16)


def _prep_attn_sample(w_in, w_o):
    d = w_in.shape[0]
    qw = N_HEADS * HEAD_DIM
    scale = HEAD_DIM ** -0.5
    wq = (w_in[:, :qw] * scale).reshape(d, N_KV, GROUP, HEAD_DIM).transpose(0, 2, 1, 3).reshape(d, qw)
    w_in_s = jnp.concatenate([wq, w_in[:, qw:]], axis=1).astype(BF16)
    wo_s = w_o.reshape(N_KV, GROUP, HEAD_DIM, w_o.shape[1]).transpose(1, 0, 2, 3).reshape(qw, w_o.shape[1])
    return w_in_s, wo_s.astype(BF16)


def kernel(x_prompt, x_sample, cache_kv, page_table, state_win, state_conv, c_prompt, c_sample, w_ada, b_ada, norm_g, conv_w_pw1, conv_b_pw1, conv_w_dw, conv_b_dw, conv_ln_g, conv_ln_b, conv_w_pw2, ffn_w_up, ffn_w_down, kv_norm_g, w_kv, w_cmp1, w_cmp2, pe_cmp, nsa_w_in, nsa_w_o, final_norm_g):
    b, t, d = x_prompt.shape
    bd, tn, _ = x_sample.shape
    depth = norm_g.shape[0]
    n_a = conv_w_pw1.shape[0]
    n_mod = b_ada.shape[0] // d
    past = page_table.shape[1] * PAGE
    hw = N_KV * HEAD_DIM
    assert bd % 8 == 0 and b <= 8 and n_mod == 6 * depth + 4

    c_all = jnp.concatenate([c_sample, c_prompt, jnp.zeros((8 - b % 8, d), F32)], axis=0)
    mods = _mods(c_all, w_ada, b_ada, n_mod)

    ffn_w = [_prep_ffn(ffn_w_up[l], ffn_w_down[l]) for l in range(depth)]
    w_pw1 = conv_w_pw1.astype(BF16)
    w_pw2 = conv_w_pw2.astype(BF16)
    w_kv_t = w_kv.T.astype(BF16)
    w_kv_perm = _prep_kv_perm(w_kv)
    cw = _prep_compress(w_cmp1, w_cmp2, pe_cmp)
    slopes = jnp.exp2(-8.0 * jnp.arange(1, N_HEADS + 1, dtype=F32) / N_HEADS)
    kblk_kv = (6 * depth) // 2
    kblk_final = (6 * depth + 2) // 2

    def trunk(x, time_major, conv_prev_tm, attn_fn_builder):
        conv_states = []
        attn_fn = None
        extra = None
        for l in range(depth):
            if l == n_a:
                attn_fn, extra = attn_fn_builder(x)
            if l < n_a:
                x, st = _conv_layer(x, None if conv_prev_tm is None else conv_prev_tm[l], mods, bd, l, time_major,
                                    norm_g[l, 0], w_pw1[l], conv_b_pw1[l], conv_w_dw[l], conv_b_dw[l],
                                    conv_ln_g[l], conv_ln_b[l], w_pw2[l])
                conv_states.append(st)
                o = wo = None
            else:
                o, wo = attn_fn(x, l)
            last = l == depth - 1
            x = _ffn_layer(x, mods, bd, l, time_major, norm_g[l, 1], ffn_w[l][0], ffn_w[l][1], o=o, wo=wo,
                           final_g=final_norm_g if last else None, n_mod_final=kblk_final)
        return x, extra, conv_states

    def from_feature_major(a, lead):
        nl = len(lead)
        a = a.reshape(lead + (-1, N_KV, HEAD_DIM, a.shape[-1]))
        return a.transpose(tuple(range(nl)) + (nl + 3, nl, nl + 1, nl + 2))

    attn_w_p = [_prep_attn_prompt(nsa_w_in[i], nsa_w_o[i]) for i in range(depth - n_a)]

    def prompt_attn_builder(x):
        kvt_pages, win_t, ks, kw, vs, vw = _kv_proj(x, mods, bd, kblk_kv, False, kv_norm_g, w_kv_t, w_kv_perm)
        ident = jnp.arange(b * t // PAGE, dtype=jnp.int32).reshape(b, t // PAGE)
        kcp, vcp, _, _ = _compress(kvt_pages, ident, cw)
        msel_t = _msel_t(t // STRIDE, t // L_SEL)

        def attn(xl, l):
            w_in_p, wo_p = attn_w_p[l - n_a]
            q, gates = _qin(xl, mods, bd, l, False, norm_g[l, 0], w_in_p)
            return _attn_prompt(slopes, q, gates, ks, kw, vs, vw, kcp, vcp, msel_t), wo_p

        extra = (from_feature_major(kvt_pages, (b * t // PAGE,)), from_feature_major(win_t, (b,)))
        return attn, extra

    y_prompt, (kv_prompt, win_prompt), conv_p = trunk(x_prompt, False, None, prompt_attn_builder)
    conv_prompt = jnp.stack([st[:, 32 - (CONV_W - 1):] for st in conv_p])

    attn_w_s = [_prep_attn_sample(nsa_w_in[i], nsa_w_o[i]) for i in range(depth - n_a)]
    cache_t = cache_kv.transpose(0, 2, 3, 4, 1).reshape(cache_kv.shape[0], 4 * hw, PAGE)
    nbuf = state_win.shape[1]
    state_win_t = state_win.transpose(0, 2, 3, 4, 1).reshape(bd, 2 * hw, nbuf)
    slope_rows = jnp.repeat(slopes, tn).reshape(N_HEADS * tn, 1)

    def sample_attn_builder(x):
        kvt = _kv_proj(x, mods, bd, kblk_kv, True, kv_norm_g, w_kv_t, None)
        kvt3 = kvt.reshape(6 * hw, tn, bd)
        kv_new = kvt3.transpose(2, 1, 0)
        _, _, kcn, vcn = _compress(cache_t, page_table, cw)
        n_s = -(-(past + tn) // L_SEL)
        msel_t = _msel_t(kcn.shape[1], -(-n_s // 8) * 8)

        def attn(xl, l):
            w_in_s, wo_s = attn_w_s[l - n_a]
            a = _qin(xl, mods, bd, l, True, norm_g[l, 0], w_in_s).transpose(1, 0, 2)
            q = a[..., :N_HEADS * HEAD_DIM]
            gates = a[..., N_HEADS * HEAD_DIM:].reshape(bd, tn, N_HEADS, 3).transpose(0, 3, 2, 1)
            gates = gates.reshape(bd, 3, N_HEADS * tn, 1)
            o = _attn_sample(page_table, cache_t, q, gates, kcn, vcn, state_win_t, kv_new, slope_rows, msel_t,
                             past)
            return o.transpose(1, 0, 2), wo_s

        n_keep = min(WINDOW, past + tn)
        win_all_t = jnp.concatenate([state_win_t, kvt3[4 * hw:].transpose(2, 0, 1)], axis=2)[:, :, -n_keep:]
        kv_sample = kvt3[:4 * hw].reshape(4, N_KV, HEAD_DIM, tn, bd).transpose(4, 3, 0, 1, 2)
        return attn, (kv_sample, from_feature_major(win_all_t, (bd,)))

    conv_prev_tm = state_conv.transpose(0, 2, 1, 3)
    y_s_tm, (kv_sample, win_sample), conv_s = trunk(x_sample.transpose(1, 0, 2), True, conv_prev_tm,
                                                    sample_attn_builder)
    y_sample = y_s_tm.transpose(1, 0, 2)
    conv_sample = jnp.stack(conv_s).transpose(0, 2, 1, 3)

    return (y_prompt, y_sample, kv_prompt, kv_sample, win_prompt, win_sample, conv_prompt, conv_sample)
```
